```python
import jax, jax.numpy as jnp
from jax import lax
import numpy as np

D_MODEL = 1024
BATCH = 8
SEQ = 4096
DEPTH = 1

A_HEADS = 8
A_HEAD_DIM = 64
A_W = A_HEADS * A_HEAD_DIM
IDX_HEADS = 8
IDX_DIM = 64
IDX_TOPK_MAX = 256
M_HEADS = 8
M_Q_RANK = 384
M_KV_RANK = 256
M_NOPE = 64
M_ROPE = 32
M_QK = M_NOPE + M_ROPE
M_V = 64
M_W = M_HEADS * M_V
P_HEADS = 8
P_NKEYS = 128
P_NEXP = P_NKEYS * P_NKEYS
P_DKEY = 256
P_TOPK = 16

ROPE_THETA = 10000.0
EPS = 1e-6
Q_BLOCK = 128
TOK_BLOCK = 128

IN_SPLITS = (A_W, A_W, A_W,
             IDX_HEADS * IDX_DIM, IDX_DIM, IDX_HEADS,
             M_Q_RANK, M_KV_RANK, M_ROPE,
             D_MODEL, D_MODEL)
IN_WIDTH = sum(IN_SPLITS)

kernel_name = "hybrid_dsa_mla_peer_adaln"


def rms_norm(x, g):
    xf = x.astype(jnp.float32)
    y = xf * lax.rsqrt(jnp.mean(xf * xf, axis=-1, keepdims=True) + EPS)
    return (y * g.astype(jnp.float32)).astype(x.dtype)


def rope(x, pos):
    d = x.shape[-1]
    half = d // 2
    inv = ROPE_THETA ** (-jnp.arange(half, dtype=jnp.float32) / half)
    ang = pos.astype(jnp.float32)[..., None] * inv
    ang = ang.reshape(ang.shape[:2] + (1,) * (x.ndim - 3) + (half,))
    cos, sin = jnp.cos(ang), jnp.sin(ang)
    xf = x.astype(jnp.float32)
    x1, x2 = xf[..., :half], xf[..., half:]
    return jnp.concatenate([x1 * cos - x2 * sin, x2 * cos + x1 * sin], axis=-1).astype(x.dtype)


def to_blocks(a, nb):
    return a.reshape((a.shape[0], nb, Q_BLOCK) + a.shape[2:]).swapaxes(0, 1)


def dsa_sparse_attention(q, k, v, qi, ki, wi):
    B, S, H, Dh = q.shape
    n_sel = min(IDX_TOPK_MAX, S // 4)
    nb = S // Q_BLOCK
    key_pos = jnp.arange(S)
    scale = Dh ** -0.5

    def block(args):
        qb, qib, wib, start = args
        t = start + jnp.arange(Q_BLOCK)
        causal = key_pos[None, :] <= t[:, None]
        dots = jnp.einsum('bqhd,bsd->bqhs', qib, ki, preferred_element_type=jnp.float32)
        score = jnp.einsum('bqhs,bqh->bqs', jax.nn.relu(dots), wib.astype(jnp.float32))
        score = jnp.where(causal[None], score, -jnp.inf)
        _, idx = lax.top_k(score, n_sel)
        valid = idx <= t[None, :, None]
        k_sel = jax.vmap(lambda kb, ib: kb[ib])(k, idx)
        v_sel = jax.vmap(lambda vb, ib: vb[ib])(v, idx)
        logits = jnp.einsum('bqhd,bqnhd->bhqn', qb, k_sel, preferred_element_type=jnp.float32) * scale
        logits = jnp.where(valid[:, None], logits, -jnp.inf)
        p = jax.nn.softmax(logits, axis=-1)
        return jnp.einsum('bhqn,bqnhd->bqhd', p.astype(v.dtype), v_sel)

    starts = jnp.arange(nb) * Q_BLOCK
    out = lax.map(block, (to_blocks(q, nb), to_blocks(qi, nb), to_blocks(wi, nb), starts))
    return out.swapaxes(0, 1).reshape(B, S, H, Dh)


def causal_block_attention(q, k, v):
    B, S, H, Dq = q.shape
    nb = S // Q_BLOCK
    key_pos = jnp.arange(S)
    scale = Dq ** -0.5

    def block(args):
        qb, start = args
        t = start + jnp.arange(Q_BLOCK)
        logits = jnp.einsum('bqhd,bshd->bhqs', qb, k, preferred_element_type=jnp.float32) * scale
        logits = jnp.where((key_pos[None, :] <= t[:, None])[None, None], logits, -jnp.inf)
        p = jax.nn.softmax(logits, axis=-1)
        return jnp.einsum('bhqs,bshd->bqhd', p.astype(v.dtype), v)

    starts = jnp.arange(nb) * Q_BLOCK
    out = lax.map(block, (to_blocks(q, nb), starts))
    return out.swapaxes(0, 1).reshape(B, S, H, v.shape[-1])


def peer(h, w_q, subkeys, u, v):
    B, S, D = h.shape
    q = (h @ w_q).reshape(B, S, P_HEADS, 2, P_DKEY // 2)
    s = jnp.einsum('bshpd,hpnd->bshpn', q, subkeys, preferred_element_type=jnp.float32)
    top_s, top_i = lax.top_k(s, P_TOPK)
    cand = (top_s[..., 0, :, None] + top_s[..., 1, None, :]).reshape(B, S, P_HEADS, P_TOPK * P_TOPK)
    cand_idx = (top_i[..., 0, :, None] * P_NKEYS + top_i[..., 1, None, :]).reshape(B, S, P_HEADS, P_TOPK * P_TOPK)
    best_s, best_pos = lax.top_k(cand, P_TOPK)
    expert = jnp.take_along_axis(cand_idx, best_pos, axis=-1)
    gate = jax.nn.softmax(best_s, axis=-1)
    T = B * S
    nb = T // TOK_BLOCK
    hb = h.reshape(nb, TOK_BLOCK, D)
    eb = expert.reshape(nb, TOK_BLOCK, P_HEADS * P_TOPK)
    gb = gate.reshape(nb, TOK_BLOCK, P_HEADS * P_TOPK)

    def block(args):
        ht, et, gt = args
        act = jax.nn.gelu(jnp.einsum('td,ted->te', ht, u[et], preferred_element_type=jnp.float32))
        coef = (gt * act).astype(v.dtype)
        return jnp.einsum('te,ted->td', coef, v[et])

    return lax.map(block, (hb, eb, gb)).reshape(B, S, D)


def hybrid_layer(x, c_act, pos, g_norm1, g_norm2, w_ada, b_ada, w_in, g_a_q, g_a_k, g_idx_k,
                 g_mq_a, w_mq_up, g_mkv_a, w_mkv_up, g_m_q, g_m_k, w_o_a, w_o_m, w_out,
                 w_peer_q, peer_subkeys, peer_u, peer_v):
    B, S, D = x.shape
    mod = (c_act @ w_ada + b_ada)[:, None, :]
    sh1, sc1, gt1, sh2, sc2, gt2 = jnp.split(mod, 6, axis=-1)

    h = rms_norm(x, g_norm1) * (1 + sc1) + sh1
    offs = [int(o) for o in np.cumsum(IN_SPLITS)[:-1]]
    qa, ka, va, qi, ki, wi, cq, ckv, kpe, ga, gm = jnp.split(h @ w_in, offs, axis=-1)

    qa = rope(rms_norm(qa.reshape(B, S, A_HEADS, A_HEAD_DIM), g_a_q), pos)
    ka = rope(rms_norm(ka.reshape(B, S, A_HEADS, A_HEAD_DIM), g_a_k), pos)
    va = va.reshape(B, S, A_HEADS, A_HEAD_DIM)
    qi = rope(qi.reshape(B, S, IDX_HEADS, IDX_DIM), pos)
    ki = rope(rms_norm(ki, g_idx_k), pos)
    wi = wi * (IDX_HEADS * IDX_DIM) ** -0.5
    ya = dsa_sparse_attention(qa, ka, va, qi, ki, wi).reshape(B, S, A_W) @ w_o_a

    qm = (rms_norm(cq, g_mq_a) @ w_mq_up).reshape(B, S, M_HEADS, M_QK)
    kv = (rms_norm(ckv, g_mkv_a) @ w_mkv_up).reshape(B, S, M_HEADS, M_NOPE + M_V)
    k_nope, vm = kv[..., :M_NOPE], kv[..., M_NOPE:]
    km = jnp.concatenate([k_nope, jnp.broadcast_to(kpe[:, :, None, :], (B, S, M_HEADS, M_ROPE))], axis=-1)
    qm = rms_norm(qm, g_m_q)
    km = rms_norm(km, g_m_k)
    qm = jnp.concatenate([qm[..., :M_NOPE], rope(qm[..., M_NOPE:], pos)], axis=-1)
    km = jnp.concatenate([km[..., :M_NOPE], rope(km[..., M_NOPE:], pos)], axis=-1)
    ym = causal_block_attention(qm, km, vm).reshape(B, S, M_W) @ w_o_m

    mixed = jax.nn.sigmoid(ga) * ya + jax.nn.sigmoid(gm) * ym
    x = x + gt1 * (mixed @ w_out)

    h2 = rms_norm(x, g_norm2) * (1 + sc2) + sh2
    x = x + gt2 * peer(h2, w_peer_q, peer_subkeys, peer_u, peer_v)
    return x


def setup_inputs(seed: int = 0) -> dict:
    key = jax.random.key(seed)
    ks = jax.random.split(key, 24)

    def nrm(k, shape, scale):
        return jax.random.normal(k, (DEPTH,) + shape, jnp.float32) * scale

    def gain(k, n):
        return 1.0 + 0.02 * jax.random.normal(k, (DEPTH, n), jnp.float32)

    x = jax.random.normal(ks[0], (BATCH, SEQ, D_MODEL), jnp.float32)
    c = jax.random.normal(ks[1], (BATCH, D_MODEL), jnp.float32)
    positions = jnp.broadcast_to(jnp.arange(SEQ, dtype=jnp.int32)[None, :], (BATCH, SEQ))
    return {
        "x": x,
        "c": c,
        "positions": positions,
        "g_norm1": gain(ks[2], D_MODEL),
        "g_norm2": gain(ks[3], D_MODEL),
        "w_ada": nrm(ks[4], (D_MODEL, 6 * D_MODEL), 0.5 * D_MODEL ** -0.5),
        "b_ada": nrm(ks[5], (6 * D_MODEL,), 0.02),
        "w_in": nrm(ks[6], (D_MODEL, IN_WIDTH), D_MODEL ** -0.5),
        "g_a_q": gain(ks[7], A_HEAD_DIM),
        "g_a_k": gain(ks[8], A_HEAD_DIM),
        "g_idx_k": gain(ks[9], IDX_DIM),
        "g_mq_a": gain(ks[10], M_Q_RANK),
        "w_mq_up": nrm(ks[11], (M_Q_RANK, M_HEADS * M_QK), M_Q_RANK ** -0.5),
        "g_mkv_a": gain(ks[12], M_KV_RANK),
        "w_mkv_up": nrm(ks[13], (M_KV_RANK, M_HEADS * (M_NOPE + M_V)), M_KV_RANK ** -0.5),
        "g_m_q": gain(ks[14], M_QK),
        "g_m_k": gain(ks[15], M_QK),
        "w_o_a": nrm(ks[16], (A_W, D_MODEL), A_W ** -0.5),
        "w_o_m": nrm(ks[17], (M_W, D_MODEL), M_W ** -0.5),
        "w_out": nrm(ks[18], (D_MODEL, D_MODEL), D_MODEL ** -0.5),
        "w_peer_q": nrm(ks[19], (D_MODEL, P_HEADS * P_DKEY), D_MODEL ** -0.5),
        "peer_subkeys": nrm(ks[20], (P_HEADS, 2, P_NKEYS, P_DKEY // 2), (P_DKEY // 2) ** -0.5),
        "peer_u": nrm(ks[21], (P_NEXP, D_MODEL), D_MODEL ** -0.5),
        "peer_v": nrm(ks[22], (P_NEXP, D_MODEL), P_HEADS ** -0.5),
    }


def reference(x, c, positions, g_norm1, g_norm2, w_ada, b_ada, w_in, g_a_q, g_a_k, g_idx_k,
              g_mq_a, w_mq_up, g_mkv_a, w_mkv_up, g_m_q, g_m_k, w_o_a, w_o_m, w_out,
              w_peer_q, peer_subkeys, peer_u, peer_v):
    c_act = jax.nn.silu(c)
    for i in range(DEPTH):
        x = hybrid_layer(x, c_act, positions, g_norm1[i], g_norm2[i], w_ada[i], b_ada[i], w_in[i],
                         g_a_q[i], g_a_k[i], g_idx_k[i], g_mq_a[i], w_mq_up[i], g_mkv_a[i],
                         w_mkv_up[i], g_m_q[i], g_m_k[i], w_o_a[i], w_o_m[i], w_out[i],
                         w_peer_q[i], peer_subkeys[i], peer_u[i], peer_v[i])
    return x
```

```python
import functools
import math

import numpy as np
import jax
import jax.numpy as jnp
from jax import lax
from jax.experimental import pallas as pl
from jax.experimental.pallas import tpu as pltpu

F32 = jnp.float32
BF16 = jnp.bfloat16
I32 = jnp.int32

A_HEADS = 8
A_HEAD_DIM = 64
A_W = A_HEADS * A_HEAD_DIM
IDX_HEADS = 8
IDX_DIM = 64
IDX_TOPK_MAX = 256
M_HEADS = 8
M_Q_RANK = 384
M_KV_RANK = 256
M_NOPE = 64
M_ROPE = 32
M_QK = M_NOPE + M_ROPE
M_V = 64
M_W = M_HEADS * M_V
P_HEADS = 8
P_NKEYS = 128
P_NEXP = P_NKEYS * P_NKEYS
P_DKEY = 256
P_TOPK = 16
ROPE_THETA = 10000.0
EPS = 1e-6

LANES = 128
M_HEAD_PAD = LANES
VMEM_LIMIT = 56 * 1024 * 1024
NEG_BIG = -1e30
INT_MIN = -2147483648

_PEER_CELLS = [(a, b) for a in range(P_TOPK) for b in range(P_TOPK) if (a + 1) * (b + 1) <= P_TOPK]


def _split_bf16(a):
    hi = a.astype(BF16)
    lo = (a - hi.astype(F32)).astype(BF16)
    return hi, lo


def _dot(a, b):
    return jnp.dot(a, b, preferred_element_type=F32)


def _dot_nt(a, b):
    return lax.dot_general(a, b, (((1,), (1,)), ((), ())), preferred_element_type=F32)


def _dot3(a, b):
    ah, al = _split_bf16(a)
    bh, bl = _split_bf16(b)
    return _dot(ah, bh) + _dot(ah, bl) + _dot(al, bh)


def _dot3_nt(a, b):
    ah, al = _split_bf16(a)
    bh, bl = _split_bf16(b)
    return _dot_nt(ah, bh) + _dot_nt(ah, bl) + _dot_nt(al, bh)


def _dot_sel(a, sel_bf16):
    ah, al = _split_bf16(a)
    return _dot(ah, sel_bf16) + _dot(al, sel_bf16)


def _adaln_kernel(c_ref, w_ref, b_ref, o_ref):
    c = c_ref[...]
    ca = c * jax.nn.sigmoid(c)
    o_ref[...] = _dot3(ca, w_ref[...]) + b_ref[...]


def _adaln(c, w_ada, b_ada):
    B, D = c.shape
    n = w_ada.shape[1] // D
    return pl.pallas_call(
        _adaln_kernel,
        grid=(n,),
        in_specs=[pl.BlockSpec((B, D), lambda j: (0, 0)),
                  pl.BlockSpec((D, D), lambda j: (0, j)),
                  pl.BlockSpec((1, D), lambda j: (0, j))],
        out_specs=pl.BlockSpec((B, D), lambda j: (0, j)),
        out_shape=jax.ShapeDtypeStruct((B, n * D), F32),
        compiler_params=pltpu.CompilerParams(dimension_semantics=("arbitrary",),
                                             vmem_limit_bytes=VMEM_LIMIT),
        name="adaln",
    )(c, w_ada, b_ada.reshape(1, -1))


def _rope_lanes(x, cos, sin_signed, first_half, half):
    n = x.shape[-1]
    up = pltpu.roll(x, n - half, 1)
    down = pltpu.roll(x, half, 1)
    partner = jnp.where(first_half, up, down)
    return x * cos + partner * sin_signed


def _inproj_kernel(x_ref, mod_ref, pos_ref, g1_ref,
                   wqkv_ref, wqi_ref, wsm_ref, wcq_ref, wckv_ref, wg_ref,
                   gaq_ref, gak_ref, gik_ref, gmqa_ref, gmkva_ref,
                   wmq_ref, wmk_ref, wmv_ref, gmq_ref, gmk_ref,
                   grp64_ref, grp64t_ref, grp128_ref, grp128t_ref,
                   inv64_ref, invm_ref,
                   qa_ref, ka_ref, va_ref, qi_ref, ki_ref, wi_ref,
                   qm_ref, km_ref, vm_ref, sga_ref, sgm_ref):
    x = x_ref[0]
    mod = mod_ref[0]
    sh1, sc1 = mod[0:1, :], mod[1:2, :]
    D = x.shape[-1]
    r = lax.rsqrt(jnp.sum(x * x, axis=-1, keepdims=True) * (1.0 / D) + EPS)
    h = (x * r * g1_ref[...]) * (1.0 + sc1) + sh1
    hb = h.astype(BF16)

    pos = pos_ref[0].astype(F32)
    lane = lax.broadcasted_iota(I32, (1, LANES), 1)
    ang = pos * inv64_ref[...]
    cos64 = jnp.cos(ang)
    first64 = (lane % 64) < 32
    sin64 = jnp.where(first64, -jnp.sin(ang), jnp.sin(ang))
    angm = pos * invm_ref[...]
    cosm = jnp.cos(angm)
    firstm = (lane % 32) < 16
    sinm = jnp.where(firstm, -jnp.sin(angm), jnp.sin(angm))

    def rope64(v):
        w = v.shape[-1] // LANES
        parts = [_rope_lanes(v[:, j * LANES:(j + 1) * LANES], cos64, sin64, first64, 32) for j in range(w)]
        return parts[0] if w == 1 else jnp.concatenate(parts, axis=-1)

    def ropem(v):
        w = v.shape[-1] // LANES
        parts = [_rope_lanes(v[:, j * LANES:(j + 1) * LANES], cosm, sinm, firstm, 16) for j in range(w)]
        return jnp.concatenate(parts, axis=-1)

    def group_norm(v, grp_ref, grpt_ref, n):
        ss = _dot_sel(v * v, grp_ref[...])
        rr = lax.rsqrt(ss * (1.0 / n) + EPS)
        return v * _dot_sel(rr, grpt_ref[...])

    qkv = _dot(hb, wqkv_ref[...])
    qa = group_norm(qkv[:, 0:A_W], grp64_ref, grp64t_ref, A_HEAD_DIM) * gaq_ref[...]
    ka = group_norm(qkv[:, A_W:2 * A_W], grp64_ref, grp64t_ref, A_HEAD_DIM) * gak_ref[...]
    qa_ref[0] = (rope64(qa) * (A_HEAD_DIM ** -0.5)).astype(BF16)
    ka_ref[0] = rope64(ka).astype(BF16)
    va_ref[0] = qkv[:, 2 * A_W:3 * A_W].astype(BF16)

    qi_ref[0] = rope64(_dot(hb, wqi_ref[...])).astype(BF16)
    sm = _dot(hb, wsm_ref[...])
    ki = jnp.where(lane < IDX_DIM, sm, 0.0)
    rk = lax.rsqrt(jnp.sum(ki * ki, axis=-1, keepdims=True) * (1.0 / IDX_DIM) + EPS)
    ki = rope64(ki * rk * gik_ref[...])
    ki = ki + pltpu.roll(ki, IDX_DIM, 1)
    ki_ref[0] = ki.astype(BF16)
    wi_ref[0] = sm[:, 96:96 + IDX_HEADS] * ((IDX_HEADS * IDX_DIM) ** -0.5)

    cq = _dot(hb, wcq_ref[...])
    rq = lax.rsqrt(jnp.sum(cq * cq, axis=-1, keepdims=True) * (1.0 / M_Q_RANK) + EPS)
    cqn = (cq * rq * gmqa_ref[...]).astype(BF16)
    qm = _dot(cqn, wmq_ref[...])
    qm = group_norm(qm, grp128_ref, grp128t_ref, M_QK) * gmq_ref[...]
    qm_ref[0] = (ropem(qm) * (M_QK ** -0.5)).astype(BF16)

    ckv = _dot(hb, wckv_ref[...])
    rkv = lax.rsqrt(jnp.sum(ckv * ckv, axis=-1, keepdims=True) * (1.0 / M_KV_RANK) + EPS)
    ckvn = (ckv * rkv * gmkva_ref[...]).astype(BF16)
    kpe = jnp.where((lane >= M_NOPE) & (lane < M_QK), sm, 0.0)
    km = _dot(ckvn, wmk_ref[...]) + jnp.concatenate([kpe] * M_HEADS, axis=-1)
    km = group_norm(km, grp128_ref, grp128t_ref, M_QK) * gmk_ref[...]
    km_ref[0] = ropem(km).astype(BF16)
    vm_ref[0] = _dot(ckvn, wmv_ref[...]).astype(BF16)

    g = _dot(hb, wg_ref[...])
    sga_ref[0] = jax.nn.sigmoid(g[:, :D]).astype(BF16)
    sgm_ref[0] = jax.nn.sigmoid(g[:, D:]).astype(BF16)


def _group_matrix(width, group):
    m = np.zeros((width, LANES), np.float32)
    m[np.arange(width), np.arange(width) // group] = 1.0
    return m


def _inproj(x, mod, positions, g_norm1, w_in, g_a_q, g_a_k, g_idx_k, g_mq_a, w_mq_up,
            g_mkv_a, w_mkv_up, g_m_q, g_m_k, tm):
    B, S, D = x.shape
    o = np.cumsum([0, A_W, A_W, A_W, IDX_HEADS * IDX_DIM, IDX_DIM, IDX_HEADS,
                   M_Q_RANK, M_KV_RANK, M_ROPE, D, D])
    wb = w_in.astype(BF16)
    wqkv = wb[:, o[0]:o[3]]
    wqi = wb[:, o[3]:o[4]]
    wsm = jnp.concatenate([wb[:, o[4]:o[5]], wb[:, o[8]:o[9]], wb[:, o[5]:o[6]],
                           jnp.zeros((D, LANES - IDX_DIM - M_ROPE - IDX_HEADS), BF16)], axis=1)
    wcq = wb[:, o[6]:o[7]]
    wckv = wb[:, o[7]:o[8]]
    wg = wb[:, o[9]:o[11]]
    wmq = jnp.pad(w_mq_up.reshape(M_Q_RANK, M_HEADS, M_QK),
                  ((0, 0), (0, 0), (0, M_HEAD_PAD - M_QK))).reshape(M_Q_RANK, M_HEADS * M_HEAD_PAD).astype(BF16)
    wkv = w_mkv_up.reshape(M_KV_RANK, M_HEADS, M_NOPE + M_V)
    wmk = jnp.pad(wkv[:, :, :M_NOPE], ((0, 0), (0, 0), (0, M_HEAD_PAD - M_NOPE))
                  ).reshape(M_KV_RANK, M_HEADS * M_HEAD_PAD).astype(BF16)
    wmv = wkv[:, :, M_NOPE:].reshape(M_KV_RANK, M_W).astype(BF16)
    gmq = jnp.tile(jnp.pad(g_m_q, (0, M_HEAD_PAD - M_QK)), M_HEADS).reshape(1, -1)
    gmk = jnp.tile(jnp.pad(g_m_k, (0, M_HEAD_PAD - M_QK)), M_HEADS).reshape(1, -1)
    gaq = jnp.tile(g_a_q, A_HEADS).reshape(1, -1)
    gak = jnp.tile(g_a_k, A_HEADS).reshape(1, -1)
    gik = jnp.pad(g_idx_k, (0, LANES - IDX_DIM)).reshape(1, -1)

    grp64 = _group_matrix(A_W, A_HEAD_DIM)
    grp128 = _group_matrix(M_HEADS * M_HEAD_PAD, M_HEAD_PAD)
    l = np.arange(LANES)
    inv64 = (ROPE_THETA ** (-(l % 32).astype(np.float64) / 32)).astype(np.float32).reshape(1, -1)
    invm = np.where((l >= M_NOPE) & (l < M_QK),
                    ROPE_THETA ** (-(l % 16).astype(np.float64) / 16), 0.0).astype(np.float32).reshape(1, -1)

    consts = [wqkv, wqi, wsm, wcq, wckv, wg, gaq, gak, gik, g_mq_a.reshape(1, -1), g_mkv_a.reshape(1, -1),
              wmq, wmk, wmv, gmq, gmk,
              jnp.asarray(grp64, BF16), jnp.asarray(grp64.T, BF16),
              jnp.asarray(grp128, BF16), jnp.asarray(grp128.T, BF16),
              jnp.asarray(inv64), jnp.asarray(invm)]

    def const_spec(a):
        return pl.BlockSpec(a.shape, lambda b, i: (0, 0))

    def tok_spec(w):
        return pl.BlockSpec((1, tm, w), lambda b, i: (b, i, 0))

    out_w = [(A_W, BF16), (A_W, BF16), (A_W, BF16), (IDX_HEADS * IDX_DIM, BF16), (LANES, BF16),
             (IDX_HEADS, F32), (M_HEADS * M_HEAD_PAD, BF16), (M_HEADS * M_HEAD_PAD, BF16), (M_W, BF16),
             (D, BF16), (D, BF16)]
    return pl.pallas_call(
        _inproj_kernel,
        grid=(B, S // tm),
        in_specs=[tok_spec(D),
                  pl.BlockSpec((1, 6, D), lambda b, i: (b, 0, 0)),
                  tok_spec(1),
                  pl.BlockSpec((1, D), lambda b, i: (0, 0))] + [const_spec(a) for a in consts],
        out_specs=[tok_spec(w) for w, _ in out_w],
        out_shape=[jax.ShapeDtypeStruct((B, S, w), dt) for w, dt in out_w],
        compiler_params=pltpu.CompilerParams(dimension_semantics=("parallel", "parallel"),
                                             vmem_limit_bytes=VMEM_LIMIT),
        name="inproj",
    )(x, mod, positions.reshape(B, S, 1), g_norm1.reshape(1, D), *consts)


def _flash_step(q, k, v, bias, m, l, acc):
    s = _dot_nt(q, k)
    if bias is not None:
        s = s + bias
    m_new = jnp.maximum(m, jnp.max(s, axis=-1, keepdims=True))
    alpha = jnp.exp(m - m_new)
    p = jnp.exp(s - m_new)
    l = alpha * l + jnp.sum(p, axis=-1, keepdims=True)
    acc = alpha * acc + _dot(p.astype(BF16), v)
    return m_new, l, acc


def _pair_masks():
    lane = lax.broadcasted_iota(I32, (1, LANES), 1)
    return lane < 64


def _sortable(x):
    b = pltpu.bitcast(x, I32)
    return b ^ ((b >> 31) & 0x7FFFFFFF)


def _dsa_kernel(qa_ref, qi_ref, wi_ref, ki_ref, ka_ref, va_ref, o_ref, key_scr, bias_scr, *, tq, n_sel, s_bits):
    i = pl.program_id(1)
    nkv = i + 1
    tk = tq
    row = i * tq + lax.broadcasted_iota(I32, (tq, 1), 0)
    col0 = lax.broadcasted_iota(I32, (1, tk), 1)
    lo_half = _pair_masks()

    qi = qi_ref[0]
    wi = wi_ref[0]
    zero = jnp.zeros((), BF16)
    qparts = []
    for h in range(IDX_HEADS):
        pair = qi[:, (h // 2) * LANES:(h // 2 + 1) * LANES]
        qparts.append(jnp.where(lo_half if h % 2 == 0 else ~lo_half, pair, zero))
    wcols = [wi[:, h:h + 1] for h in range(IDX_HEADS)]

    def score_body(c, carry):
        ks = pl.multiple_of(c * tk, tk)
        kic = ki_ref[0, pl.ds(ks, tk), :]
        acc = jnp.zeros((tq, tk), F32)
        for h in range(IDX_HEADS):
            acc = acc + jnp.maximum(_dot_nt(qparts[h], kic), 0.0) * wcols[h]
        acc = jnp.where(acc == 0.0, 0.0, acc)
        key = jnp.where(col0 + ks <= row, _sortable(acc), INT_MIN)
        key_scr[:, pl.ds(ks, tk)] = key
        return carry

    lax.fori_loop(0, nkv, score_body, 0)

    def count(pred):
        def body(c, cnt):
            ks = pl.multiple_of(c * tk, tk)
            m = pred(key_scr[:, pl.ds(ks, tk)], col0 + ks).astype(I32)
            part = m[:, 0:LANES]
            for j in range(1, tk // LANES):
                part = part + m[:, j * LANES:(j + 1) * LANES]
            return cnt + part
        cnt = lax.fori_loop(0, nkv, body, jnp.zeros((tq, LANES), I32))
        return jnp.sum(cnt, axis=-1, keepdims=True)

    def value_bit(step, tu):
        bit = jnp.left_shift(jnp.int32(1), 31 - step)
        cand = (tu | bit) ^ INT_MIN
        ok = count(lambda k, col: k >= cand) >= n_sel
        return jnp.where(ok, tu | bit, tu)

    tu = lax.fori_loop(0, 32, value_bit, jnp.zeros((tq, 1), I32))
    thr = tu ^ INT_MIN
    cnt_gt = count(lambda k, col: k > thr)
    cnt_ge = count(lambda k, col: k >= thr)
    need = n_sel - cnt_gt
    excess = ((cnt_ge - cnt_gt) > need) & (thr != INT_MIN)

    def tie_search(_):
        def index_bit(step, ju):
            cj = ju | jnp.left_shift(jnp.int32(1), s_bits - 1 - step)
            ok = count(lambda k, col: (k == thr) & (col < cj)) < need
            return jnp.where(ok, cj, ju)
        ju = lax.fori_loop(0, s_bits, index_bit, jnp.zeros((tq, 1), I32))
        return jnp.where(excess, ju, jnp.int32(1 << s_bits))

    any_excess = jnp.max(excess.astype(I32)) > 0
    jcut = lax.cond(any_excess, tie_search, lambda _: jnp.full((tq, 1), 1 << s_bits, I32), 0)

    def bias_body(c, carry):
        ks = pl.multiple_of(c * tk, tk)
        k = key_scr[:, pl.ds(ks, tk)]
        col = col0 + ks
        sel = ((k > thr) | ((k == thr) & (col <= jcut))) & (col <= row)
        bias_scr[:, pl.ds(ks, tk)] = jnp.where(sel, 0.0, NEG_BIG)
        return carry

    lax.fori_loop(0, nkv, bias_body, 0)

    qa = qa_ref[0]
    for hp in range(A_HEADS // 2):
        qpair = qa[:, hp * LANES:(hp + 1) * LANES]
        q0 = jnp.where(lo_half, qpair, zero)
        q1 = jnp.where(lo_half, zero, qpair)

        def att_body(c, carry, hp=hp, q0=q0, q1=q1):
            m0, l0, a0, m1, l1, a1 = carry
            ks = pl.multiple_of(c * tk, tk)
            kc = ka_ref[0, pl.ds(ks, tk), hp * LANES:(hp + 1) * LANES]
            vc = va_ref[0, pl.ds(ks, tk), hp * LANES:(hp + 1) * LANES]
            bias = bias_scr[:, pl.ds(ks, tk)]
            m0, l0, a0 = _flash_step(q0, kc, vc, bias, m0, l0, a0)
            m1, l1, a1 = _flash_step(q1, kc, vc, bias, m1, l1, a1)
            return m0, l0, a0, m1, l1, a1

        mi = jnp.full((tq, 1), NEG_BIG, F32)
        li = jnp.zeros((tq, 1), F32)
        ai = jnp.zeros((tq, LANES), F32)
        m0, l0, a0, m1, l1, a1 = lax.fori_loop(0, nkv, att_body, (mi, li, ai, mi, li, ai))
        out = jnp.where(lo_half, a0 / l0, a1 / l1)
        o_ref[0, :, hp * LANES:(hp + 1) * LANES] = out.astype(BF16)


def _dsa(qa, qi, wi, ki, ka, va, tq):
    B, S, _ = qa.shape
    n_sel = min(IDX_TOPK_MAX, S // 4)
    s_bits = int(math.log2(S))
    assert 1 << s_bits == S
    kern = functools.partial(_dsa_kernel, tq=tq, n_sel=n_sel, s_bits=s_bits)
    blk = lambda w: pl.BlockSpec((1, tq, w), lambda b, i: (b, i, 0))
    full = lambda w: pl.BlockSpec((1, S, w), lambda b, i: (b, 0, 0))
    return pl.pallas_call(
        kern,
        grid=(B, S // tq),
        in_specs=[blk(A_W), blk(IDX_HEADS * IDX_DIM), blk(IDX_HEADS), full(LANES), full(A_W), full(A_W)],
        out_specs=blk(A_W),
        out_shape=jax.ShapeDtypeStruct((B, S, A_W), BF16),
        scratch_shapes=[pltpu.VMEM((tq, S), I32), pltpu.VMEM((tq, S), F32)],
        compiler_params=pltpu.CompilerParams(dimension_semantics=("parallel", "arbitrary"),
                                             vmem_limit_bytes=VMEM_LIMIT),
        name="dsa",
    )(qa, qi, wi, ki, ka, va)


def _mla_kernel(q_ref, k_ref, v_ref, o_ref, *, tq):
    i = pl.program_id(1)
    tk = tq
    lo_half = _pair_masks()
    q = q_ref[0]
    rowi = lax.broadcasted_iota(I32, (tq, tk), 0)
    coli = lax.broadcasted_iota(I32, (tq, tk), 1)
    diag_bias = jnp.where(coli <= rowi, 0.0, NEG_BIG)

    for hp in range(M_HEADS // 2):
        qh = [q[:, (2 * hp + e) * M_HEAD_PAD:(2 * hp + e + 1) * M_HEAD_PAD] for e in range(2)]

        def step(c, carry, bias, hp=hp, qh=qh):
            m0, l0, a0, m1, l1, a1 = carry
            ks = pl.multiple_of(c * tk, tk)
            k0 = k_ref[0, pl.ds(ks, tk), (2 * hp) * M_HEAD_PAD:(2 * hp + 1) * M_HEAD_PAD]
            k1 = k_ref[0, pl.ds(ks, tk), (2 * hp + 1) * M_HEAD_PAD:(2 * hp + 2) * M_HEAD_PAD]
            vc = v_ref[0, pl.ds(ks, tk), hp * LANES:(hp + 1) * LANES]
            m0, l0, a0 = _flash_step(qh[0], k0, vc, bias, m0, l0, a0)
            m1, l1, a1 = _flash_step(qh[1], k1, vc, bias, m1, l1, a1)
            return m0, l0, a0, m1, l1, a1

        mi = jnp.full((tq, 1), NEG_BIG, F32)
        li = jnp.zeros((tq, 1), F32)
        ai = jnp.zeros((tq, LANES), F32)
        carry = lax.fori_loop(0, i, functools.partial(step, bias=None), (mi, li, ai, mi, li, ai))
        m0, l0, a0, m1, l1, a1 = step(i, carry, diag_bias)
        out = jnp.where(lo_half, a0 / l0, a1 / l1)
        o_ref[0, :, hp * LANES:(hp + 1) * LANES] = out.astype(BF16)


def _mla(qm, km, vm, tq):
    B, S, W = qm.shape
    kern = functools.partial(_mla_kernel, tq=tq)
    return pl.pallas_call(
        kern,
        grid=(B, S // tq),
        in_specs=[pl.BlockSpec((1, tq, W), lambda b, i: (b, i, 0)),
                  pl.BlockSpec((1, S, W), lambda b, i: (b, 0, 0)),
                  pl.BlockSpec((1, S, M_W), lambda b, i: (b, 0, 0))],
        out_specs=pl.BlockSpec((1, tq, M_W), lambda b, i: (b, i, 0)),
        out_shape=jax.ShapeDtypeStruct((B, S, M_W), BF16),
        compiler_params=pltpu.CompilerParams(dimension_semantics=("parallel", "arbitrary"),
                                             vmem_limit_bytes=VMEM_LIMIT),
        name="mla",
    )(qm, km, vm)


def _merge_kernel(x_ref, mod_ref, ya_ref, ym_ref, sga_ref, sgm_ref, woa_ref, wom_ref, wout_ref, g2_ref,
                  x1_ref, h2_ref):
    x = x_ref[0]
    mod = mod_ref[0]
    gt1, sh2, sc2 = mod[2:3, :], mod[3:4, :], mod[4:5, :]
    ya = _dot(ya_ref[0], woa_ref[...])
    ym = _dot(ym_ref[0], wom_ref[...])
    mixed = sga_ref[0].astype(F32) * ya + sgm_ref[0].astype(F32) * ym
    x1 = x + gt1 * _dot(mixed.astype(BF16), wout_ref[...])
    x1_ref[0] = x1
    D = x.shape[-1]
    r = lax.rsqrt(jnp.sum(x1 * x1, axis=-1, keepdims=True) * (1.0 / D) + EPS)
    h2_ref[0] = ((x1 * r * g2_ref[...]) * (1.0 + sc2) + sh2).astype(BF16)


def _merge(x, mod, ya, ym, sga, sgm, w_o_a, w_o_m, w_out, g_norm2, tm):
    B, S, D = x.shape
    tok = lambda w: pl.BlockSpec((1, tm, w), lambda b, i: (b, i, 0))
    cst = lambda a: pl.BlockSpec(a.shape, lambda b, i: (0, 0))
    woa, wom, wout = w_o_a.astype(BF16), w_o_m.astype(BF16), w_out.astype(BF16)
    g2 = g_norm2.reshape(1, D)
    return pl.pallas_call(
        _merge_kernel,
        grid=(B, S // tm),
        in_specs=[tok(D), pl.BlockSpec((1, 6, D), lambda b, i: (b, 0, 0)), tok(A_W), tok(M_W), tok(D), tok(D),
                  cst(woa), cst(wom), cst(wout), cst(g2)],
        out_specs=[tok(D), tok(D)],
        out_shape=[jax.ShapeDtypeStruct((B, S, D), F32), jax.ShapeDtypeStruct((B, S, D), BF16)],
        compiler_params=pltpu.CompilerParams(dimension_semantics=("parallel", "parallel"),
                                             vmem_limit_bytes=VMEM_LIMIT),
        name="merge",
    )(x, mod, ya, ym, sga, sgm, woa, wom, wout, g2)


def _top_rows(x, k):
    n = x.shape[0]
    iota = lax.broadcasted_iota(I32, x.shape, 0)
    vals = []
    for _ in range(k):
        m = jnp.max(x, axis=0, keepdims=True)
        idx = jnp.min(jnp.where(x == m, iota, n), axis=0, keepdims=True)
        vals.append(m)
        x = jnp.where(iota == idx, -jnp.inf, x)
    return vals


def _router_kernel(h2_ref, wq_ref, sk_ref, s_ref, e_ref, thr_ref, cand_scr):
    tm = h2_ref.shape[0]
    q = _dot(h2_ref[...], wq_ref[...])
    ncell = len(_PEER_CELLS)
    cand_scr[...] = jnp.full(cand_scr.shape, -jnp.inf, F32)
    for h in range(P_HEADS):
        tops = []
        sts = []
        for p in range(2):
            j = 2 * h + p
            st = _dot3_nt(sk_ref[j], q[:, j * LANES:(j + 1) * LANES])
            s_ref[j * P_NKEYS:(j + 1) * P_NKEYS, :] = st
            sts.append(st)
            tops.append(_top_rows(st, P_TOPK))
        for r, (a, b) in enumerate(_PEER_CELLS):
            cand_scr[r:r + 1, :] = tops[0][a] + tops[1][b]
        best = _top_rows(cand_scr[...], P_TOPK)
        mx = best[0]
        z = jnp.zeros_like(mx)
        for v in best:
            z = z + jnp.exp(v - mx)
        thr_ref[h:h + 1, :] = best[P_TOPK - 1]
        inv_z = 1.0 / z
        e_ref[(2 * h) * P_NKEYS:(2 * h + 1) * P_NKEYS, :] = jnp.exp(sts[0] - tops[0][0])
        e_ref[(2 * h + 1) * P_NKEYS:(2 * h + 2) * P_NKEYS, :] = jnp.exp(sts[1] - tops[1][0]) * inv_z
    del ncell


def _router(h2, w_peer_q, peer_subkeys, tm):
    T, D = h2.shape
    wq = w_peer_q.astype(BF16)
    sk = peer_subkeys.reshape(P_HEADS * 2, P_NKEYS, P_DKEY // 2)
    nrow = P_HEADS * 2 * P_NKEYS
    ncell_pad = -(-len(_PEER_CELLS) // 8) * 8
    return pl.pallas_call(
        _router_kernel,
        grid=(T // tm,),
        in_specs=[pl.BlockSpec((tm, D), lambda i: (i, 0)),
                  pl.BlockSpec(wq.shape, lambda i: (0, 0)),
                  pl.BlockSpec(sk.shape, lambda i: (0, 0, 0))],
        out_specs=[pl.BlockSpec((nrow, tm), lambda i: (0, i)),
                   pl.BlockSpec((nrow, tm), lambda i: (0, i)),
                   pl.BlockSpec((P_HEADS, tm), lambda i: (0, i))],
        out_shape=[jax.ShapeDtypeStruct((nrow, T), F32), jax.ShapeDtypeStruct((nrow, T), F32),
                   jax.ShapeDtypeStruct((P_HEADS, T), F32)],
        scratch_shapes=[pltpu.VMEM((ncell_pad, tm), F32)],
        compiler_params=pltpu.CompilerParams(dimension_semantics=("parallel",),
                                             vmem_limit_bytes=VMEM_LIMIT),
        name="router",
    )(h2, wq, sk)


def _gelu_tanh(x):
    return 0.5 * x * (1.0 + jnp.tanh(0.7978845608028654 * (x + 0.044715 * (x * x * x))))


def _experts_kernel(h2_ref, s_ref, e_ref, thr_ref, u_ref, vt_ref, x1_ref, mod_ref, o_ref, acc_scr, *, rows_per_step):
    j = pl.program_id(1)
    nj = pl.num_programs(1)

    @pl.when(j == 0)
    def _():
        acc_scr[...] = jnp.zeros(acc_scr.shape, F32)

    act = _gelu_tanh(_dot_nt(u_ref[...], h2_ref[...]))
    coefs = []
    for r in range(rows_per_step):
        i_row = j * rows_per_step + r
        g = None
        for h in range(P_HEADS):
            base0 = (2 * h) * P_NKEYS
            base1 = (2 * h + 1) * P_NKEYS
            s0 = s_ref[pl.ds(base0 + i_row, 1), :]
            e0 = e_ref[pl.ds(base0 + i_row, 1), :]
            s1 = s_ref[base1:base1 + P_NKEYS, :]
            e1 = e_ref[base1:base1 + P_NKEYS, :]
            t = jnp.where(s0 + s1 >= thr_ref[h:h + 1, :], e0 * e1, 0.0)
            g = t if g is None else g + t
        coefs.append(g * act[r * P_NKEYS:(r + 1) * P_NKEYS, :])
    coef = coefs[0] if rows_per_step == 1 else jnp.concatenate(coefs, axis=0)
    acc_scr[...] += _dot(vt_ref[...], coef.astype(BF16))

    @pl.when(j == nj - 1)
    def _():
        gt2 = mod_ref[0][5:6, :]
        o_ref[...] = x1_ref[...] + gt2 * acc_scr[...].T


def _experts(h2, s_t, e_t, thr_t, peer_u, peer_v, x1, mod, seq, tm, te):
    T, D = h2.shape
    ub = peer_u.astype(BF16)
    vt = peer_v.astype(BF16).T
    nrow = s_t.shape[0]
    blocks_per_seq = seq // tm
    kern = functools.partial(_experts_kernel, rows_per_step=te // P_NKEYS)
    return pl.pallas_call(
        kern,
        grid=(T // tm, P_NEXP // te),
        in_specs=[pl.BlockSpec((tm, D), lambda i, j: (i, 0)),
                  pl.BlockSpec((nrow, tm), lambda i, j: (0, i)),
                  pl.BlockSpec((nrow, tm), lambda i, j: (0, i)),
                  pl.BlockSpec((P_HEADS, tm), lambda i, j: (0, i)),
                  pl.BlockSpec((te, D), lambda i, j: (j, 0)),
                  pl.BlockSpec((D, te), lambda i, j: (0, j)),
                  pl.BlockSpec((tm, D), lambda i, j: (i, 0)),
                  pl.BlockSpec((1, 6, D), lambda i, j: (i // blocks_per_seq, 0, 0))],
        out_specs=pl.BlockSpec((tm, D), lambda i, j: (i, 0)),
        out_shape=jax.ShapeDtypeStruct((T, D), F32),
        scratch_shapes=[pltpu.VMEM((D, tm), F32)],
        compiler_params=pltpu.CompilerParams(dimension_semantics=("parallel", "arbitrary"),
                                             vmem_limit_bytes=VMEM_LIMIT),
        name="experts",
    )(h2, s_t, e_t, thr_t, ub, vt, x1, mod)


def _tile(n, pref):
    t = min(n, pref)
    assert n % t == 0
    return t


def _layer(x, mod, positions, g_norm1, g_norm2, w_in, g_a_q, g_a_k, g_idx_k, g_mq_a, w_mq_up, g_mkv_a,
           w_mkv_up, g_m_q, g_m_k, w_o_a, w_o_m, w_out, w_peer_q, peer_subkeys, peer_u, peer_v):
    B, S, D = x.shape
    T = B * S
    qa, ka, va, qi, ki, wi, qm, km, vm, sga, sgm = _inproj(
        x, mod, positions, g_norm1, w_in, g_a_q, g_a_k, g_idx_k, g_mq_a, w_mq_up, g_mkv_a, w_mkv_up,
        g_m_q, g_m_k, _tile(S, 256))
    ya = _dsa(qa, qi, wi, ki, ka, va, _tile(S, 256))
    ym = _mla(qm, km, vm, _tile(S, 256))
    x1, h2 = _merge(x, mod, ya, ym, sga, sgm, w_o_a, w_o_m, w_out, g_norm2, _tile(S, 512))
    h2f = h2.reshape(T, D)
    s_t, e_t, thr_t = _router(h2f, w_peer_q, peer_subkeys, _tile(T, 256))
    tm_e = _tile(S, 512)
    out = _experts(h2f, s_t, e_t, thr_t, peer_u, peer_v, x1.reshape(T, D), mod, S, tm_e, 256)
    return out.reshape(B, S, D)


def kernel(x, c, positions, g_norm1, g_norm2, w_ada, b_ada, w_in, g_a_q, g_a_k, g_idx_k, g_mq_a, w_mq_up,
           g_mkv_a, w_mkv_up, g_m_q, g_m_k, w_o_a, w_o_m, w_out, w_peer_q, peer_subkeys, peer_u, peer_v):
    depth = w_in.shape[0]
    B, D = c.shape
    for i in range(depth):
        mod = _adaln(c, w_ada[i], b_ada[i]).reshape(B, 6, D)
        x = _layer(x, mod, positions, g_norm1[i], g_norm2[i], w_in[i], g_a_q[i], g_a_k[i], g_idx_k[i],
                   g_mq_a[i], w_mq_up[i], g_mkv_a[i], w_mkv_up[i], g_m_q[i], g_m_k[i], w_o_a[i], w_o_m[i],
                   w_out[i], w_peer_q[i], peer_subkeys[i], peer_u[i], peer_v[i])
    return x
```

```python
import functools
import math

import numpy as np
import jax
import jax.numpy as jnp
from jax import lax
from jax.experimental import pallas as pl
from jax.experimental.pallas import tpu as pltpu

F32 = jnp.float32
BF16 = jnp.bfloat16
I32 = jnp.int32

A_HEADS = 8
A_HEAD_DIM = 64
A_W = A_HEADS * A_HEAD_DIM
IDX_HEADS = 8
IDX_DIM = 64
IDX_TOPK_MAX = 256
M_HEADS = 8
M_Q_RANK = 384
M_KV_RANK = 256
M_NOPE = 64
M_ROPE = 32
M_QK = M_NOPE + M_ROPE
M_V = 64
M_W = M_HEADS * M_V
P_HEADS = 8
P_NKEYS = 128
P_NEXP = P_NKEYS * P_NKEYS
P_DKEY = 256
P_TOPK = 16
ROPE_THETA = 10000.0
EPS = 1e-6

LANES = 128
M_HEAD_PAD = LANES
VMEM_LIMIT = 56 * 1024 * 1024
NEG_BIG = -1e30
LOG2E = 1.4426950408889634
INT_MIN = -2147483648

_PEER_CELLS = [(a, b) for a in range(P_TOPK) for b in range(P_TOPK) if (a + 1) * (b + 1) <= P_TOPK]


def _split_bf16(a):
    hi = a.astype(BF16)
    lo = (a - hi.astype(F32)).astype(BF16)
    return hi, lo


def _dot(a, b):
    return jnp.dot(a, b, preferred_element_type=F32)


def _dot_nt(a, b):
    return lax.dot_general(a, b, (((1,), (1,)), ((), ())), preferred_element_type=F32)


def _dot3(a, b):
    ah, al = _split_bf16(a)
    bh, bl = _split_bf16(b)
    return _dot(ah, bh) + _dot(ah, bl) + _dot(al, bh)


def _dot3_nt(a, b):
    ah, al = _split_bf16(a)
    bh, bl = _split_bf16(b)
    return _dot_nt(ah, bh) + _dot_nt(ah, bl) + _dot_nt(al, bh)


def _dot_sel(a, sel_bf16):
    ah, al = _split_bf16(a)
    return _dot(ah, sel_bf16) + _dot(al, sel_bf16)


def _adaln_kernel(c_ref, w_ref, b_ref, o_ref):
    c = c_ref[...]
    ca = c * jax.nn.sigmoid(c)
    o_ref[...] = _dot3(ca, w_ref[...]) + b_ref[...]


def _adaln(c, w_ada, b_ada):
    B, D = c.shape
    n = w_ada.shape[1] // D
    return pl.pallas_call(
        _adaln_kernel,
        grid=(n,),
        in_specs=[pl.BlockSpec((B, D), lambda j: (0, 0)),
                  pl.BlockSpec((D, D), lambda j: (0, j)),
                  pl.BlockSpec((1, D), lambda j: (0, j))],
        out_specs=pl.BlockSpec((B, D), lambda j: (0, j)),
        out_shape=jax.ShapeDtypeStruct((B, n * D), F32),
        compiler_params=pltpu.CompilerParams(dimension_semantics=("arbitrary",),
                                             vmem_limit_bytes=VMEM_LIMIT),
        name="adaln",
    )(c, w_ada, b_ada.reshape(1, -1))


def _rope_lanes(x, cos, sin_signed, first_half, half):
    n = x.shape[-1]
    up = pltpu.roll(x, n - half, 1)
    down = pltpu.roll(x, half, 1)
    partner = jnp.where(first_half, up, down)
    return x * cos + partner * sin_signed


def _inproj_kernel(x_ref, mod_ref, pos_ref, g1_ref,
                   wqk_ref, wvt_ref, wqi_ref, wsm_ref, wwit_ref, wcq_ref, wckv_ref, wg_ref,
                   gaq_ref, gak_ref, gik_ref, gmqa_ref, gmkva_ref,
                   wmq_ref, wmk_ref, wmvt_ref, gmq_ref, gmk_ref,
                   grp64_ref, grp64t_ref, grp128_ref, grp128t_ref,
                   inv64_ref, invm_ref,
                   qa_ref, ka_ref, vat_ref, qi_ref, ki_ref, wit_ref,
                   qm_ref, km_ref, vmt_ref, sga_ref, sgm_ref):
    x = x_ref[0]
    mod = mod_ref[0]
    sh1, sc1 = mod[0:1, :], mod[1:2, :]
    D = x.shape[-1]
    r = lax.rsqrt(jnp.sum(x * x, axis=-1, keepdims=True) * (1.0 / D) + EPS)
    h = (x * r * g1_ref[...]) * (1.0 + sc1) + sh1
    hb = h.astype(BF16)

    pos = pos_ref[0].astype(F32)
    lane = lax.broadcasted_iota(I32, (1, LANES), 1)
    ang = pos * inv64_ref[...]
    cos64 = jnp.cos(ang)
    first64 = (lane % 64) < 32
    sin64 = jnp.where(first64, -jnp.sin(ang), jnp.sin(ang))
    angm = pos * invm_ref[...]
    cosm = jnp.cos(angm)
    firstm = (lane % 32) < 16
    sinm = jnp.where(firstm, -jnp.sin(angm), jnp.sin(angm))

    def rope64(v):
        w = v.shape[-1] // LANES
        parts = [_rope_lanes(v[:, j * LANES:(j + 1) * LANES], cos64, sin64, first64, 32) for j in range(w)]
        return parts[0] if w == 1 else jnp.concatenate(parts, axis=-1)

    def ropem(v):
        w = v.shape[-1] // LANES
        parts = [_rope_lanes(v[:, j * LANES:(j + 1) * LANES], cosm, sinm, firstm, 16) for j in range(w)]
        return jnp.concatenate(parts, axis=-1)

    def group_norm(v, grp_ref, grpt_ref, n):
        ss = _dot_sel(v * v, grp_ref[...])
        rr = lax.rsqrt(ss * (1.0 / n) + EPS)
        return v * _dot_sel(rr, grpt_ref[...])

    qk = _dot(hb, wqk_ref[...])
    qa = group_norm(qk[:, 0:A_W], grp64_ref, grp64t_ref, A_HEAD_DIM) * gaq_ref[...]
    ka = group_norm(qk[:, A_W:2 * A_W], grp64_ref, grp64t_ref, A_HEAD_DIM) * gak_ref[...]
    qa_ref[0] = (rope64(qa) * (A_HEAD_DIM ** -0.5 * LOG2E)).astype(BF16)
    ka_ref[0] = rope64(ka).astype(BF16)
    vat_ref[0, 0] = _dot_nt(wvt_ref[...], hb).astype(BF16)

    qi_ref[0] = rope64(_dot(hb, wqi_ref[...])).astype(BF16)
    sm = _dot(hb, wsm_ref[...])
    ki = jnp.where(lane < IDX_DIM, sm, 0.0)
    rk = lax.rsqrt(jnp.sum(ki * ki, axis=-1, keepdims=True) * (1.0 / IDX_DIM) + EPS)
    ki = rope64(ki * rk * gik_ref[...])
    ki = ki + pltpu.roll(ki, IDX_DIM, 1)
    ki_ref[0] = ki.astype(BF16)
    wit_ref[0] = _dot_nt(wwit_ref[...], hb) * ((IDX_HEADS * IDX_DIM) ** -0.5)

    cq = _dot(hb, wcq_ref[...])
    rq = lax.rsqrt(jnp.sum(cq * cq, axis=-1, keepdims=True) * (1.0 / M_Q_RANK) + EPS)
    cqn = (cq * rq * gmqa_ref[...]).astype(BF16)
    qm = _dot(cqn, wmq_ref[...])
    qm = group_norm(qm, grp128_ref, grp128t_ref, M_QK) * gmq_ref[...]
    qm_ref[0] = (ropem(qm) * (M_QK ** -0.5 * LOG2E)).astype(BF16)

    ckv = _dot(hb, wckv_ref[...])
    rkv = lax.rsqrt(jnp.sum(ckv * ckv, axis=-1, keepdims=True) * (1.0 / M_KV_RANK) + EPS)
    ckvn = (ckv * rkv * gmkva_ref[...]).astype(BF16)
    kpe = jnp.where((lane >= M_NOPE) & (lane < M_QK), sm, 0.0)
    km = _dot(ckvn, wmk_ref[...]) + jnp.concatenate([kpe] * M_HEADS, axis=-1)
    km = group_norm(km, grp128_ref, grp128t_ref, M_QK) * gmk_ref[...]
    km_ref[0] = ropem(km).astype(BF16)
    vmt_ref[0, 0] = _dot_nt(wmvt_ref[...], ckvn).astype(BF16)

    g = _dot(hb, wg_ref[...])
    sga_ref[0] = jax.nn.sigmoid(g[:, :D]).astype(BF16)
    sgm_ref[0] = jax.nn.sigmoid(g[:, D:]).astype(BF16)


def _group_matrix(width, group):
    m = np.zeros((width, LANES), np.float32)
    m[np.arange(width), np.arange(width) // group] = 1.0
    return m


def _inproj(x, mod, positions, g_norm1, w_in, g_a_q, g_a_k, g_idx_k, g_mq_a, w_mq_up,
            g_mkv_a, w_mkv_up, g_m_q, g_m_k, tm):
    B, S, D = x.shape
    o = np.cumsum([0, A_W, A_W, A_W, IDX_HEADS * IDX_DIM, IDX_DIM, IDX_HEADS,
                   M_Q_RANK, M_KV_RANK, M_ROPE, D, D])
    wb = w_in.astype(BF16)
    wqk = wb[:, o[0]:o[2]]
    wvt = wb[:, o[2]:o[3]].T
    wqi = wb[:, o[3]:o[4]]
    wsm = jnp.concatenate([wb[:, o[4]:o[5]], wb[:, o[8]:o[9]],
                           jnp.zeros((D, LANES - IDX_DIM - M_ROPE), BF16)], axis=1)
    wwit = wb[:, o[5]:o[6]].T
    wcq = wb[:, o[6]:o[7]]
    wckv = wb[:, o[7]:o[8]]
    wg = wb[:, o[9]:o[11]]
    wmq = jnp.pad(w_mq_up.reshape(M_Q_RANK, M_HEADS, M_QK),
                  ((0, 0), (0, 0), (0, M_HEAD_PAD - M_QK))).reshape(M_Q_RANK, M_HEADS * M_HEAD_PAD).astype(BF16)
    wkv = w_mkv_up.reshape(M_KV_RANK, M_HEADS, M_NOPE + M_V)
    wmk = jnp.pad(wkv[:, :, :M_NOPE], ((0, 0), (0, 0), (0, M_HEAD_PAD - M_NOPE))
                  ).reshape(M_KV_RANK, M_HEADS * M_HEAD_PAD).astype(BF16)
    wmvt = wkv[:, :, M_NOPE:].reshape(M_KV_RANK, M_W).astype(BF16).T
    gmq = jnp.tile(jnp.pad(g_m_q, (0, M_HEAD_PAD - M_QK)), M_HEADS).reshape(1, -1)
    gmk = jnp.tile(jnp.pad(g_m_k, (0, M_HEAD_PAD - M_QK)), M_HEADS).reshape(1, -1)
    gaq = jnp.tile(g_a_q, A_HEADS).reshape(1, -1)
    gak = jnp.tile(g_a_k, A_HEADS).reshape(1, -1)
    gik = jnp.pad(g_idx_k, (0, LANES - IDX_DIM)).reshape(1, -1)

    grp64 = _group_matrix(A_W, A_HEAD_DIM)
    grp128 = _group_matrix(M_HEADS * M_HEAD_PAD, M_HEAD_PAD)
    l = np.arange(LANES)
    inv64 = (ROPE_THETA ** (-(l % 32).astype(np.float64) / 32)).astype(np.float32).reshape(1, -1)
    invm = np.where((l >= M_NOPE) & (l < M_QK),
                    ROPE_THETA ** (-(l % 16).astype(np.float64) / 16), 0.0).astype(np.float32).reshape(1, -1)

    consts = [wqk, wvt, wqi, wsm, wwit, wcq, wckv, wg, gaq, gak, gik, g_mq_a.reshape(1, -1),
              g_mkv_a.reshape(1, -1), wmq, wmk, wmvt, gmq, gmk,
              jnp.asarray(grp64, BF16), jnp.asarray(grp64.T, BF16),
              jnp.asarray(grp128, BF16), jnp.asarray(grp128.T, BF16),
              jnp.asarray(inv64), jnp.asarray(invm)]

    def const_spec(a):
        return pl.BlockSpec(a.shape, lambda b, i: (0, 0))

    def tok_spec(w):
        return pl.BlockSpec((1, tm, w), lambda b, i: (b, i, 0))

    nb = S // tm
    tok = lambda w, dt: (tok_spec(w), jax.ShapeDtypeStruct((B, S, w), dt))
    slab = lambda r: (pl.BlockSpec((1, 1, r, tm), lambda b, i: (b, i, 0, 0)),
                      jax.ShapeDtypeStruct((B, nb, r, tm), BF16))
    outs = [tok(A_W, BF16), tok(A_W, BF16), slab(A_W), tok(IDX_HEADS * IDX_DIM, BF16), tok(LANES, BF16),
            (pl.BlockSpec((1, IDX_HEADS, tm), lambda b, i: (b, 0, i)),
             jax.ShapeDtypeStruct((B, IDX_HEADS, S), F32)),
            tok(M_HEADS * M_HEAD_PAD, BF16), tok(M_HEADS * M_HEAD_PAD, BF16), slab(M_W),
            tok(D, BF16), tok(D, BF16)]
    return pl.pallas_call(
        _inproj_kernel,
        grid=(B, nb),
        in_specs=[tok_spec(D),
                  pl.BlockSpec((1, 6, D), lambda b, i: (b, 0, 0)),
                  tok_spec(1),
                  pl.BlockSpec((1, D), lambda b, i: (0, 0))] + [const_spec(a) for a in consts],
        out_specs=[s for s, _ in outs],
        out_shape=[t for _, t in outs],
        compiler_params=pltpu.CompilerParams(dimension_semantics=("parallel", "parallel"),
                                             vmem_limit_bytes=VMEM_LIMIT),
        name="inproj",
    )(x, mod, positions.reshape(B, S, 1), g_norm1.reshape(1, D), *consts)


def _flash_chunk_t(qs, ks, vts, bias, ms, ls, s_scr, p_scr, acc_scr, dv):
    nh = len(qs)
    mx = []
    for h in range(nh):
        s = _dot_nt(ks[h], qs[h])
        if bias is not None:
            s = s + bias
        s_scr[h] = s
        mx.append(jnp.max(s, axis=0, keepdims=True))
    new_m, new_l, alphas = [], [], []
    for h in range(nh):
        m_new = jnp.maximum(ms[h], mx[h])
        alpha = jnp.exp2(ms[h] - m_new)
        p = jnp.exp2(s_scr[h] - m_new)
        new_l.append(alpha * ls[h] + jnp.sum(p, axis=0, keepdims=True))
        p_scr[h] = p.astype(BF16)
        new_m.append(m_new)
        alphas.append(alpha)
    for h in range(nh):
        rows = slice(h * dv, (h + 1) * dv)
        acc_scr[rows, :] = alphas[h] * acc_scr[rows, :] + _dot(vts[h], p_scr[h])
    return tuple(new_m), tuple(new_l)


def _pair_masks():
    lane = lax.broadcasted_iota(I32, (1, LANES), 1)
    return lane < 64


def _attn_finish(acc_ref, ls, o_ref, dv):
    for h, l in enumerate(ls):
        acc_ref[h * dv:(h + 1) * dv, :] = acc_ref[h * dv:(h + 1) * dv, :] * (1.0 / l)
    o_ref[0] = acc_ref[...].T.astype(BF16)


def _sortable(x):
    b = pltpu.bitcast(x, I32)
    return b ^ ((b >> 31) & 0x7FFFFFFF)


def _dsa_kernel(qa_ref, qi_ref, wit_ref, ki_ref, ka_ref, vat_ref, o_ref, key_scr, bias_scr, acc_scr,
                s_scr, p_scr, *, tq, n_sel, s_bits):
    i = pl.program_id(1)
    nkv = i + 1
    tk = tq
    qidx = i * tq + lax.broadcasted_iota(I32, (1, tq), 1)
    kidx0 = lax.broadcasted_iota(I32, (tk, 1), 0)
    lo_half = _pair_masks()
    zero = jnp.zeros((), BF16)

    def head_parts(x):
        parts = []
        for h in range(A_HEADS):
            pair = x[:, (h // 2) * LANES:(h // 2 + 1) * LANES]
            parts.append(jnp.where(lo_half, pair, zero) if h % 2 == 0 else jnp.where(lo_half, zero, pair))
        return parts

    qparts = head_parts(qi_ref[0])
    wit = wit_ref[0]
    wrows = [wit[h:h + 1, :] for h in range(IDX_HEADS)]

    def score_body(c, carry):
        ks = pl.multiple_of(c * tk, tk)
        kic = ki_ref[0, pl.ds(ks, tk), :]
        acc = jnp.zeros((tk, tq), F32)
        for h in range(IDX_HEADS):
            acc = acc + jnp.maximum(_dot_nt(kic, qparts[h]), 0.0) * wrows[h]
        acc = jnp.where(acc == 0.0, 0.0, acc)
        key_scr[pl.ds(ks, tk), :] = jnp.where(kidx0 + ks <= qidx, _sortable(acc), INT_MIN)
        return carry

    lax.fori_loop(0, nkv, score_body, 0)

    def count(pred):
        def body(c, cnt):
            ks = pl.multiple_of(c * tk, tk)
            m = pred(key_scr[pl.ds(ks, tk), :], kidx0 + ks).astype(I32)
            return cnt + jnp.sum(m.reshape(tk // 8, 8, tq), axis=0)
        cnt = lax.fori_loop(0, nkv, body, jnp.zeros((8, tq), I32))
        return jnp.sum(cnt, axis=0, keepdims=True)

    def search_cond(st):
        step, _, done, _ = st
        return (step < 32) & (jnp.min(done) == 0)

    def search_body(st):
        step, tu, done, thr_hit = st
        bit = jnp.left_shift(jnp.int32(1), 31 - step)
        cand = (tu | bit) ^ INT_MIN
        cnt = count(lambda k, kidx: k >= cand)
        hit = (cnt == n_sel) & (done == 0)
        thr_hit = jnp.where(hit, cand, thr_hit)
        tu = jnp.where(cnt >= n_sel, tu | bit, tu)
        return step + 1, tu, jnp.where(hit, 1, done), thr_hit

    zrow = jnp.zeros((1, tq), I32)
    _, tu, done, thr_hit = lax.while_loop(search_cond, search_body, (jnp.int32(0), zrow, zrow, zrow))
    thr = jnp.where(done == 1, thr_hit, tu ^ INT_MIN)

    s_all = jnp.int32(1 << s_bits)

    def tie_limits(_):
        cnt_gt = count(lambda k, kidx: k > thr)
        cnt_ge = count(lambda k, kidx: k >= thr)
        need = n_sel - cnt_gt
        excess = (done == 0) & ((cnt_ge - cnt_gt) > need) & (thr != INT_MIN)

        def tie_search(_):
            def index_bit(step, ju):
                cj = ju | jnp.left_shift(jnp.int32(1), s_bits - 1 - step)
                ok = count(lambda k, kidx: (k == thr) & (kidx < cj)) < need
                return jnp.where(ok, cj, ju)
            return lax.fori_loop(0, s_bits, index_bit, zrow)

        ju = lax.cond(jnp.max(excess.astype(I32)) > 0, tie_search, lambda _: zrow, 0)
        return jnp.where(excess, ju, jnp.where(thr == INT_MIN, -1, s_all))

    jcut = lax.cond(jnp.min(done) == 0, tie_limits, lambda _: jnp.full((1, tq), 1 << s_bits, I32), 0)

    def bias_body(c, carry):
        ks = pl.multiple_of(c * tk, tk)
        k = key_scr[pl.ds(ks, tk), :]
        tie = jnp.where(kidx0 + ks <= jcut, 0.0, NEG_BIG)
        bias_scr[pl.ds(ks, tk), :] = jnp.where(k > thr, 0.0, jnp.where(k == thr, tie, NEG_BIG))
        return carry

    lax.fori_loop(0, nkv, bias_body, 0)

    qh = head_parts(qa_ref[0])
    acc_scr[...] = jnp.zeros(acc_scr.shape, F32)

    def att_body(c, carry):
        ms, ls = carry
        ks = pl.multiple_of(c * tk, tk)
        bias = bias_scr[pl.ds(ks, tk), :]
        kcs = [ka_ref[0, pl.ds(ks, tk), (h // 2) * LANES:(h // 2 + 1) * LANES] for h in range(A_HEADS)]
        vts = [vat_ref[0, c, h * A_HEAD_DIM:(h + 1) * A_HEAD_DIM, :] for h in range(A_HEADS)]
        return _flash_chunk_t(qh, kcs, vts, bias, ms, ls, s_scr, p_scr, acc_scr, A_HEAD_DIM)

    m0 = tuple(jnp.full((1, tq), NEG_BIG, F32) for _ in range(A_HEADS))
    l0 = tuple(jnp.zeros((1, tq), F32) for _ in range(A_HEADS))
    _, ls = lax.fori_loop(0, nkv, att_body, (m0, l0))
    _attn_finish(acc_scr, ls, o_ref, A_HEAD_DIM)


def _dsa(qa, qi, wit, ki, ka, vat, tq):
    B, S, _ = qa.shape
    n_sel = min(IDX_TOPK_MAX, S // 4)
    s_bits = int(math.log2(S))
    assert 1 << s_bits == S
    kern = functools.partial(_dsa_kernel, tq=tq, n_sel=n_sel, s_bits=s_bits)
    blk = lambda w: pl.BlockSpec((1, tq, w), lambda b, i: (b, i, 0))
    full = lambda w: pl.BlockSpec((1, S, w), lambda b, i: (b, 0, 0))
    return pl.pallas_call(
        kern,
        grid=(B, S // tq),
        in_specs=[blk(A_W), blk(IDX_HEADS * IDX_DIM),
                  pl.BlockSpec((1, IDX_HEADS, tq), lambda b, i: (b, 0, i)),
                  full(LANES), full(A_W),
                  pl.BlockSpec((1, S // tq, A_W, tq), lambda b, i: (b, 0, 0, 0))],
        out_specs=blk(A_W),
        out_shape=jax.ShapeDtypeStruct((B, S, A_W), BF16),
        scratch_shapes=[pltpu.VMEM((S, tq), I32), pltpu.VMEM((S, tq), F32), pltpu.VMEM((A_W, tq), F32),
                        pltpu.VMEM((A_HEADS, tq, tq), F32), pltpu.VMEM((A_HEADS, tq, tq), BF16)],
        compiler_params=pltpu.CompilerParams(dimension_semantics=("parallel", "arbitrary"),
                                             vmem_limit_bytes=VMEM_LIMIT),
        name="dsa",
    )(qa, qi, wit, ki, ka, vat)


def _mla_kernel(q_ref, k_ref, vt_ref, o_ref, acc_scr, s_scr, p_scr, *, tq):
    i = pl.program_id(1)
    tk = tq
    q = q_ref[0]
    qh = [q[:, h * M_HEAD_PAD:(h + 1) * M_HEAD_PAD] for h in range(M_HEADS)]
    kidx = lax.broadcasted_iota(I32, (tk, tq), 0)
    qidx = lax.broadcasted_iota(I32, (tk, tq), 1)
    diag_bias = jnp.where(kidx <= qidx, 0.0, NEG_BIG)
    acc_scr[...] = jnp.zeros(acc_scr.shape, F32)

    def step(c, carry, bias):
        ms, ls = carry
        ks = pl.multiple_of(c * tk, tk)
        kcs = [k_ref[0, pl.ds(ks, tk), h * M_HEAD_PAD:(h + 1) * M_HEAD_PAD] for h in range(M_HEADS)]
        vts = [vt_ref[0, c, h * M_V:(h + 1) * M_V, :] for h in range(M_HEADS)]
        return _flash_chunk_t(qh, kcs, vts, bias, ms, ls, s_scr, p_scr, acc_scr, M_V)

    m0 = tuple(jnp.full((1, tq), NEG_BIG, F32) for _ in range(M_HEADS))
    l0 = tuple(jnp.zeros((1, tq), F32) for _ in range(M_HEADS))
    carry = lax.fori_loop(0, i, functools.partial(step, bias=None), (m0, l0))
    _, ls = step(i, carry, diag_bias)
    _attn_finish(acc_scr, ls, o_ref, M_V)


def _mla(qm, km, vmt, tq):
    B, S, W = qm.shape
    kern = functools.partial(_mla_kernel, tq=tq)
    return pl.pallas_call(
        kern,
        grid=(B, S // tq),
        in_specs=[pl.BlockSpec((1, tq, W), lambda b, i: (b, i, 0)),
                  pl.BlockSpec((1, S, W), lambda b, i: (b, 0, 0)),
                  pl.BlockSpec((1, S // tq, M_W, tq), lambda b, i: (b, 0, 0, 0))],
        out_specs=pl.BlockSpec((1, tq, M_W), lambda b, i: (b, i, 0)),
        out_shape=jax.ShapeDtypeStruct((B, S, M_W), BF16),
        scratch_shapes=[pltpu.VMEM((M_W, tq), F32), pltpu.VMEM((M_HEADS, tq, tq), F32),
                        pltpu.VMEM((M_HEADS, tq, tq), BF16)],
        compiler_params=pltpu.CompilerParams(dimension_semantics=("parallel", "arbitrary"),
                                             vmem_limit_bytes=VMEM_LIMIT),
        name="mla",
    )(qm, km, vmt)


def _merge_kernel(x_ref, mod_ref, ya_ref, ym_ref, sga_ref, sgm_ref, woa_ref, wom_ref, wout_ref, g2_ref,
                  x1_ref, h2_ref):
    x = x_ref[0]
    mod = mod_ref[0]
    gt1, sh2, sc2 = mod[2:3, :], mod[3:4, :], mod[4:5, :]
    ya = _dot(ya_ref[0], woa_ref[...])
    ym = _dot(ym_ref[0], wom_ref[...])
    mixed = sga_ref[0].astype(F32) * ya + sgm_ref[0].astype(F32) * ym
    x1 = x + gt1 * _dot(mixed.astype(BF16), wout_ref[...])
    x1_ref[0] = x1
    D = x.shape[-1]
    r = lax.rsqrt(jnp.sum(x1 * x1, axis=-1, keepdims=True) * (1.0 / D) + EPS)
    h2_ref[0] = ((x1 * r * g2_ref[...]) * (1.0 + sc2) + sh2).astype(BF16)


def _merge(x, mod, ya, ym, sga, sgm, w_o_a, w_o_m, w_out, g_norm2, tm):
    B, S, D = x.shape
    tok = lambda w: pl.BlockSpec((1, tm, w), lambda b, i: (b, i, 0))
    cst = lambda a: pl.BlockSpec(a.shape, lambda b, i: (0, 0))
    woa, wom, wout = w_o_a.astype(BF16), w_o_m.astype(BF16), w_out.astype(BF16)
    g2 = g_norm2.reshape(1, D)
    return pl.pallas_call(
        _merge_kernel,
        grid=(B, S // tm),
        in_specs=[tok(D), pl.BlockSpec((1, 6, D), lambda b, i: (b, 0, 0)), tok(A_W), tok(M_W), tok(D), tok(D),
                  cst(woa), cst(wom), cst(wout), cst(g2)],
        out_specs=[tok(D), tok(D)],
        out_shape=[jax.ShapeDtypeStruct((B, S, D), F32), jax.ShapeDtypeStruct((B, S, D), BF16)],
        compiler_params=pltpu.CompilerParams(dimension_semantics=("parallel", "parallel"),
                                             vmem_limit_bytes=VMEM_LIMIT),
        name="merge",
    )(x, mod, ya, ym, sga, sgm, woa, wom, wout, g2)


def _top_rows(x, k):
    n = x.shape[0]
    iota = lax.broadcasted_iota(I32, x.shape, 0)
    vals = []
    for _ in range(k):
        m = jnp.max(x, axis=0, keepdims=True)
        idx = jnp.min(jnp.where(x == m, iota, n), axis=0, keepdims=True)
        vals.append(m)
        x = jnp.where(iota == idx, -jnp.inf, x)
    return vals


def _router_kernel(h2_ref, wq_ref, sk_ref, s_ref, e_ref, thr_ref, cand_scr):
    tm = h2_ref.shape[0]
    q = _dot(h2_ref[...], wq_ref[...])
    ncell = len(_PEER_CELLS)
    cand_scr[...] = jnp.full(cand_scr.shape, -jnp.inf, F32)
    for h in range(P_HEADS):
        tops = []
        sts = []
        for p in range(2):
            j = 2 * h + p
            st = _dot3_nt(sk_ref[j], q[:, j * LANES:(j + 1) * LANES])
            s_ref[j * P_NKEYS:(j + 1) * P_NKEYS, :] = st
            sts.append(st)
            tops.append(_top_rows(st, P_TOPK))
        for r, (a, b) in enumerate(_PEER_CELLS):
            cand_scr[r:r + 1, :] = tops[0][a] + tops[1][b]
        best = _top_rows(cand_scr[...], P_TOPK)
        mx = best[0]
        z = jnp.zeros_like(mx)
        for v in best:
            z = z + jnp.exp(v - mx)
        thr_ref[h:h + 1, :] = best[P_TOPK - 1]
        inv_z = 1.0 / z
        e_ref[(2 * h) * P_NKEYS:(2 * h + 1) * P_NKEYS, :] = jnp.exp(sts[0] - tops[0][0])
        e_ref[(2 * h + 1) * P_NKEYS:(2 * h + 2) * P_NKEYS, :] = jnp.exp(sts[1] - tops[1][0]) * inv_z
    del ncell


def _router(h2, w_peer_q, peer_subkeys, tm):
    T, D = h2.shape
    wq = w_peer_q.astype(BF16)
    sk = peer_subkeys.reshape(P_HEADS * 2, P_NKEYS, P_DKEY // 2)
    nrow = P_HEADS * 2 * P_NKEYS
    ncell_pad = -(-len(_PEER_CELLS) // 8) * 8
    return pl.pallas_call(
        _router_kernel,
        grid=(T // tm,),
        in_specs=[pl.BlockSpec((tm, D), lambda i: (i, 0)),
                  pl.BlockSpec(wq.shape, lambda i: (0, 0)),
                  pl.BlockSpec(sk.shape, lambda i: (0, 0, 0))],
        out_specs=[pl.BlockSpec((nrow, tm), lambda i: (0, i)),
                   pl.BlockSpec((nrow, tm), lambda i: (0, i)),
                   pl.BlockSpec((P_HEADS, tm), lambda i: (0, i))],
        out_shape=[jax.ShapeDtypeStruct((nrow, T), F32), jax.ShapeDtypeStruct((nrow, T), F32),
                   jax.ShapeDtypeStruct((P_HEADS, T), F32)],
        scratch_shapes=[pltpu.VMEM((ncell_pad, tm), F32)],
        compiler_params=pltpu.CompilerParams(dimension_semantics=("parallel",),
                                             vmem_limit_bytes=VMEM_LIMIT),
        name="router",
    )(h2, wq, sk)


def _gelu_tanh(x):
    return 0.5 * x * (1.0 + jnp.tanh(0.7978845608028654 * (x + 0.044715 * (x * x * x))))


def _experts_kernel(h2_ref, s_ref, e_ref, thr_ref, u_ref, vt_ref, x1_ref, mod_ref, o_ref, acc_scr, *, rows_per_step):
    j = pl.program_id(1)
    nj = pl.num_programs(1)

    @pl.when(j == 0)
    def _():
        acc_scr[...] = jnp.zeros(acc_scr.shape, F32)

    act = _gelu_tanh(_dot_nt(u_ref[...], h2_ref[...]))
    coefs = []
    for r in range(rows_per_step):
        i_row = j * rows_per_step + r
        g = None
        for h in range(P_HEADS):
            base0 = (2 * h) * P_NKEYS
            base1 = (2 * h + 1) * P_NKEYS
            s0 = s_ref[pl.ds(base0 + i_row, 1), :]
            e0 = e_ref[pl.ds(base0 + i_row, 1), :]
            s1 = s_ref[base1:base1 + P_NKEYS, :]
            e1 = e_ref[base1:base1 + P_NKEYS, :]
            t = jnp.where(s0 + s1 >= thr_ref[h:h + 1, :], e0 * e1, 0.0)
            g = t if g is None else g + t
        coefs.append(g * act[r * P_NKEYS:(r + 1) * P_NKEYS, :])
    coef = coefs[0] if rows_per_step == 1 else jnp.concatenate(coefs, axis=0)
    acc_scr[...] += _dot(vt_ref[...], coef.astype(BF16))

    @pl.when(j == nj - 1)
    def _():
        gt2 = mod_ref[0][5:6, :]
        o_ref[...] = x1_ref[...] + gt2 * acc_scr[...].T


def _experts(h2, s_t, e_t, thr_t, peer_u, peer_v, x1, mod, seq, tm, te):
    T, D = h2.shape
    ub = peer_u.astype(BF16)
    vt = peer_v.astype(BF16).T
    nrow = s_t.shape[0]
    blocks_per_seq = seq // tm
    kern = functools.partial(_experts_kernel, rows_per_step=te // P_NKEYS)
    return pl.pallas_call(
        kern,
        grid=(T // tm, P_NEXP // te),
        in_specs=[pl.BlockSpec((tm, D), lambda i, j: (i, 0)),
                  pl.BlockSpec((nrow, tm), lambda i, j: (0, i)),
                  pl.BlockSpec((nrow, tm), lambda i, j: (0, i)),
                  pl.BlockSpec((P_HEADS, tm), lambda i, j: (0, i)),
                  pl.BlockSpec((te, D), lambda i, j: (j, 0)),
                  pl.BlockSpec((D, te), lambda i, j: (0, j)),
                  pl.BlockSpec((tm, D), lambda i, j: (i, 0)),
                  pl.BlockSpec((1, 6, D), lambda i, j: (i // blocks_per_seq, 0, 0))],
        out_specs=pl.BlockSpec((tm, D), lambda i, j: (i, 0)),
        out_shape=jax.ShapeDtypeStruct((T, D), F32),
        scratch_shapes=[pltpu.VMEM((D, tm), F32)],
        compiler_params=pltpu.CompilerParams(dimension_semantics=("parallel", "arbitrary"),
                                             vmem_limit_bytes=VMEM_LIMIT),
        name="experts",
    )(h2, s_t, e_t, thr_t, ub, vt, x1, mod)


def _tile(n, pref):
    t = min(n, pref)
    assert n % t == 0
    return t


def _layer(x, mod, positions, g_norm1, g_norm2, w_in, g_a_q, g_a_k, g_idx_k, g_mq_a, w_mq_up, g_mkv_a,
           w_mkv_up, g_m_q, g_m_k, w_o_a, w_o_m, w_out, w_peer_q, peer_subkeys, peer_u, peer_v):
    B, S, D = x.shape
    T = B * S
    tq = _tile(S, 256)
    qa, ka, vat, qi, ki, wit, qm, km, vmt, sga, sgm = _inproj(
        x, mod, positions, g_norm1, w_in, g_a_q, g_a_k, g_idx_k, g_mq_a, w_mq_up, g_mkv_a, w_mkv_up,
        g_m_q, g_m_k, tq)
    ya = _dsa(qa, qi, wit, ki, ka, vat, tq)
    ym = _mla(qm, km, vmt, tq)
    x1, h2 = _merge(x, mod, ya, ym, sga, sgm, w_o_a, w_o_m, w_out, g_norm2, _tile(S, 512))
    h2f = h2.reshape(T, D)
    s_t, e_t, thr_t = _router(h2f, w_peer_q, peer_subkeys, _tile(T, 256))
    tm_e = _tile(S, 512)
    out = _experts(h2f, s_t, e_t, thr_t, peer_u, peer_v, x1.reshape(T, D), mod, S, tm_e, 256)
    return out.reshape(B, S, D)


def kernel(x, c, positions, g_norm1, g_norm2, w_ada, b_ada, w_in, g_a_q, g_a_k, g_idx_k, g_mq_a, w_mq_up,
           g_mkv_a, w_mkv_up, g_m_q, g_m_k, w_o_a, w_o_m, w_out, w_peer_q, peer_subkeys, peer_u, peer_v):
    depth = w_in.shape[0]
    B, D = c.shape
    for i in range(depth):
        mod = _adaln(c, w_ada[i], b_ada[i]).reshape(B, 6, D)
        x = _layer(x, mod, positions, g_norm1[i], g_norm2[i], w_in[i], g_a_q[i], g_a_k[i], g_idx_k[i],
                   g_mq_a[i], w_mq_up[i], g_mkv_a[i], w_mkv_up[i], g_m_q[i], g_m_k[i], w_o_a[i], w_o_m[i],
                   w_out[i], w_peer_q[i], peer_subkeys[i], peer_u[i], peer_v[i])
    return x
```

```python
import functools
import math

import numpy as np
import jax
import jax.numpy as jnp
from jax import lax
from jax.experimental import pallas as pl
from jax.experimental.pallas import tpu as pltpu

F32 = jnp.float32
BF16 = jnp.bfloat16
I32 = jnp.int32

A_HEADS = 8
A_HEAD_DIM = 64
A_W = A_HEADS * A_HEAD_DIM
IDX_HEADS = 8
IDX_DIM = 64
IDX_TOPK_MAX = 256
M_HEADS = 8
M_Q_RANK = 384
M_KV_RANK = 256
M_NOPE = 64
M_ROPE = 32
M_QK = M_NOPE + M_ROPE
M_V = 64
M_W = M_HEADS * M_V
P_HEADS = 8
P_NKEYS = 128
P_NEXP = P_NKEYS * P_NKEYS
P_DKEY = 256
P_TOPK = 16
ROPE_THETA = 10000.0
EPS = 1e-6

LANES = 128
M_HEAD_PAD = LANES
VMEM_LIMIT = 56 * 1024 * 1024
NEG_BIG = -1e30
LOG2E = 1.4426950408889634
INT_MIN = -2147483648

_PEER_CELLS = [(a, b) for a in range(P_TOPK) for b in range(P_TOPK) if (a + 1) * (b + 1) <= P_TOPK]


def _split_bf16(a):
    hi = a.astype(BF16)
    lo = (a - hi.astype(F32)).astype(BF16)
    return hi, lo


def _dot(a, b):
    return jnp.dot(a, b, preferred_element_type=F32)


def _dot_nt(a, b):
    return lax.dot_general(a, b, (((1,), (1,)), ((), ())), preferred_element_type=F32)


def _dot3(a, b):
    ah, al = _split_bf16(a)
    bh, bl = _split_bf16(b)
    return _dot(ah, bh) + _dot(ah, bl) + _dot(al, bh)


def _dot3_nt(a, b):
    ah, al = _split_bf16(a)
    bh, bl = _split_bf16(b)
    return _dot_nt(ah, bh) + _dot_nt(ah, bl) + _dot_nt(al, bh)


def _dot_sel(a, sel_bf16):
    ah, al = _split_bf16(a)
    return _dot(ah, sel_bf16) + _dot(al, sel_bf16)


def _adaln_kernel(c_ref, w_ref, b_ref, o_ref):
    c = c_ref[...]
    ca = c * jax.nn.sigmoid(c)
    o_ref[...] = _dot3(ca, w_ref[...]) + b_ref[...]


def _adaln(c, w_ada, b_ada):
    B, D = c.shape
    n = w_ada.shape[1] // D
    return pl.pallas_call(
        _adaln_kernel,
        grid=(n,),
        in_specs=[pl.BlockSpec((B, D), lambda j: (0, 0)),
                  pl.BlockSpec((D, D), lambda j: (0, j)),
                  pl.BlockSpec((1, D), lambda j: (0, j))],
        out_specs=pl.BlockSpec((B, D), lambda j: (0, j)),
        out_shape=jax.ShapeDtypeStruct((B, n * D), F32),
        compiler_params=pltpu.CompilerParams(dimension_semantics=("arbitrary",),
                                             vmem_limit_bytes=VMEM_LIMIT),
        name="adaln",
    )(c, w_ada, b_ada.reshape(1, -1))


def _rope_lanes(x, cos, sin_signed, first_half, half):
    n = x.shape[-1]
    up = pltpu.roll(x, n - half, 1)
    down = pltpu.roll(x, half, 1)
    partner = jnp.where(first_half, up, down)
    return x * cos + partner * sin_signed


def _inproj_kernel(x_ref, mod_ref, pos_ref, g1_ref,
                   wqk_ref, wvt_ref, wqi_ref, wsm_ref, wwit_ref, wcq_ref, wckv_ref, wg_ref,
                   gaq_ref, gak_ref, gik_ref, gmqa_ref, gmkva_ref,
                   wmq_ref, wmk_ref, wmvt_ref, gmq_ref, gmk_ref,
                   grp64_ref, grp64t_ref, grp128_ref, grp128t_ref,
                   inv64_ref, invm_ref,
                   qa_ref, ka_ref, vat_ref, qi_ref, ki_ref, wit_ref,
                   qm_ref, km_ref, vmt_ref, sga_ref, sgm_ref):
    x = x_ref[0]
    mod = mod_ref[0]
    sh1, sc1 = mod[0:1, :], mod[1:2, :]
    D = x.shape[-1]
    r = lax.rsqrt(jnp.sum(x * x, axis=-1, keepdims=True) * (1.0 / D) + EPS)
    h = (x * r * g1_ref[...]) * (1.0 + sc1) + sh1
    hb = h.astype(BF16)

    pos = pos_ref[0].astype(F32)
    lane = lax.broadcasted_iota(I32, (1, LANES), 1)
    ang = pos * inv64_ref[...]
    cos64 = jnp.cos(ang)
    first64 = (lane % 64) < 32
    sin64 = jnp.where(first64, -jnp.sin(ang), jnp.sin(ang))
    angm = pos * invm_ref[...]
    cosm = jnp.cos(angm)
    firstm = (lane % 32) < 16
    sinm = jnp.where(firstm, -jnp.sin(angm), jnp.sin(angm))

    def rope64(v):
        w = v.shape[-1] // LANES
        parts = [_rope_lanes(v[:, j * LANES:(j + 1) * LANES], cos64, sin64, first64, 32) for j in range(w)]
        return parts[0] if w == 1 else jnp.concatenate(parts, axis=-1)

    def ropem(v):
        w = v.shape[-1] // LANES
        parts = [_rope_lanes(v[:, j * LANES:(j + 1) * LANES], cosm, sinm, firstm, 16) for j in range(w)]
        return jnp.concatenate(parts, axis=-1)

    def group_norm(v, grp_ref, grpt_ref, n):
        ss = _dot_sel(v * v, grp_ref[...])
        rr = lax.rsqrt(ss * (1.0 / n) + EPS)
        return v * _dot_sel(rr, grpt_ref[...])

    qk = _dot(hb, wqk_ref[...])
    qa = group_norm(qk[:, 0:A_W], grp64_ref, grp64t_ref, A_HEAD_DIM) * gaq_ref[...]
    ka = group_norm(qk[:, A_W:2 * A_W], grp64_ref, grp64t_ref, A_HEAD_DIM) * gak_ref[...]
    qa_ref[0] = (rope64(qa) * (A_HEAD_DIM ** -0.5 * LOG2E)).astype(BF16)
    ka_ref[0] = rope64(ka).astype(BF16)
    vat_ref[0, 0] = _dot_nt(wvt_ref[...], hb).astype(BF16)

    qi_ref[0] = rope64(_dot(hb, wqi_ref[...])).astype(BF16)
    sm = _dot(hb, wsm_ref[...])
    ki = jnp.where(lane < IDX_DIM, sm, 0.0)
    rk = lax.rsqrt(jnp.sum(ki * ki, axis=-1, keepdims=True) * (1.0 / IDX_DIM) + EPS)
    ki = rope64(ki * rk * gik_ref[...])
    ki = ki + pltpu.roll(ki, IDX_DIM, 1)
    ki_ref[0] = ki.astype(BF16)
    wit_ref[0] = _dot_nt(wwit_ref[...], hb) * ((IDX_HEADS * IDX_DIM) ** -0.5)

    cq = _dot(hb, wcq_ref[...])
    rq = lax.rsqrt(jnp.sum(cq * cq, axis=-1, keepdims=True) * (1.0 / M_Q_RANK) + EPS)
    cqn = (cq * rq * gmqa_ref[...]).astype(BF16)
    qm = _dot(cqn, wmq_ref[...])
    qm = group_norm(qm, grp128_ref, grp128t_ref, M_QK) * gmq_ref[...]
    qm_ref[0] = (ropem(qm) * (M_QK ** -0.5 * LOG2E)).astype(BF16)

    ckv = _dot(hb, wckv_ref[...])
    rkv = lax.rsqrt(jnp.sum(ckv * ckv, axis=-1, keepdims=True) * (1.0 / M_KV_RANK) + EPS)
    ckvn = (ckv * rkv * gmkva_ref[...]).astype(BF16)
    kpe = jnp.where((lane >= M_NOPE) & (lane < M_QK), sm, 0.0)
    km = _dot(ckvn, wmk_ref[...]) + jnp.concatenate([kpe] * M_HEADS, axis=-1)
    km = group_norm(km, grp128_ref, grp128t_ref, M_QK) * gmk_ref[...]
    km_ref[0] = ropem(km).astype(BF16)
    vmt_ref[0, 0] = _dot_nt(wmvt_ref[...], ckvn).astype(BF16)

    g = _dot(hb, wg_ref[...])
    sga_ref[0] = jax.nn.sigmoid(g[:, :D]).astype(BF16)
    sgm_ref[0] = jax.nn.sigmoid(g[:, D:]).astype(BF16)


def _group_matrix(width, group):
    m = np.zeros((width, LANES), np.float32)
    m[np.arange(width), np.arange(width) // group] = 1.0
    return m


def _inproj(x, mod, positions, g_norm1, w_in, g_a_q, g_a_k, g_idx_k, g_mq_a, w_mq_up,
            g_mkv_a, w_mkv_up, g_m_q, g_m_k, tm):
    B, S, D = x.shape
    o = np.cumsum([0, A_W, A_W, A_W, IDX_HEADS * IDX_DIM, IDX_DIM, IDX_HEADS,
                   M_Q_RANK, M_KV_RANK, M_ROPE, D, D])
    wb = w_in.astype(BF16)
    wqk = wb[:, o[0]:o[2]]
    wvt = wb[:, o[2]:o[3]].T
    wqi = wb[:, o[3]:o[4]]
    wsm = jnp.concatenate([wb[:, o[4]:o[5]], wb[:, o[8]:o[9]],
                           jnp.zeros((D, LANES - IDX_DIM - M_ROPE), BF16)], axis=1)
    wwit = wb[:, o[5]:o[6]].T
    wcq = wb[:, o[6]:o[7]]
    wckv = wb[:, o[7]:o[8]]
    wg = wb[:, o[9]:o[11]]
    wmq = jnp.pad(w_mq_up.reshape(M_Q_RANK, M_HEADS, M_QK),
                  ((0, 0), (0, 0), (0, M_HEAD_PAD - M_QK))).reshape(M_Q_RANK, M_HEADS * M_HEAD_PAD).astype(BF16)
    wkv = w_mkv_up.reshape(M_KV_RANK, M_HEADS, M_NOPE + M_V)
    wmk = jnp.pad(wkv[:, :, :M_NOPE], ((0, 0), (0, 0), (0, M_HEAD_PAD - M_NOPE))
                  ).reshape(M_KV_RANK, M_HEADS * M_HEAD_PAD).astype(BF16)
    wmvt = wkv[:, :, M_NOPE:].reshape(M_KV_RANK, M_W).astype(BF16).T
    gmq = jnp.tile(jnp.pad(g_m_q, (0, M_HEAD_PAD - M_QK)), M_HEADS).reshape(1, -1)
    gmk = jnp.tile(jnp.pad(g_m_k, (0, M_HEAD_PAD - M_QK)), M_HEADS).reshape(1, -1)
    gaq = jnp.tile(g_a_q, A_HEADS).reshape(1, -1)
    gak = jnp.tile(g_a_k, A_HEADS).reshape(1, -1)
    gik = jnp.pad(g_idx_k, (0, LANES - IDX_DIM)).reshape(1, -1)

    grp64 = _group_matrix(A_W, A_HEAD_DIM)
    grp128 = _group_matrix(M_HEADS * M_HEAD_PAD, M_HEAD_PAD)
    l = np.arange(LANES)
    inv64 = (ROPE_THETA ** (-(l % 32).astype(np.float64) / 32)).astype(np.float32).reshape(1, -1)
    invm = np.where((l >= M_NOPE) & (l < M_QK),
                    ROPE_THETA ** (-(l % 16).astype(np.float64) / 16), 0.0).astype(np.float32).reshape(1, -1)

    consts = [wqk, wvt, wqi, wsm, wwit, wcq, wckv, wg, gaq, gak, gik, g_mq_a.reshape(1, -1),
              g_mkv_a.reshape(1, -1), wmq, wmk, wmvt, gmq, gmk,
              jnp.asarray(grp64, BF16), jnp.asarray(grp64.T, BF16),
              jnp.asarray(grp128, BF16), jnp.asarray(grp128.T, BF16),
              jnp.asarray(inv64), jnp.asarray(invm)]

    def const_spec(a):
        return pl.BlockSpec(a.shape, lambda b, i: (0, 0))

    def tok_spec(w):
        return pl.BlockSpec((1, tm, w), lambda b, i: (b, i, 0))

    nb = S // tm
    tok = lambda w, dt: (tok_spec(w), jax.ShapeDtypeStruct((B, S, w), dt))
    slab = lambda r: (pl.BlockSpec((1, 1, r, tm), lambda b, i: (b, i, 0, 0)),
                      jax.ShapeDtypeStruct((B, nb, r, tm), BF16))
    outs = [tok(A_W, BF16), tok(A_W, BF16), slab(A_W), tok(IDX_HEADS * IDX_DIM, BF16), tok(LANES, BF16),
            (pl.BlockSpec((1, IDX_HEADS, tm), lambda b, i: (b, 0, i)),
             jax.ShapeDtypeStruct((B, IDX_HEADS, S), F32)),
            tok(M_HEADS * M_HEAD_PAD, BF16), tok(M_HEADS * M_HEAD_PAD, BF16), slab(M_W),
            tok(D, BF16), tok(D, BF16)]
    return pl.pallas_call(
        _inproj_kernel,
        grid=(B, nb),
        in_specs=[tok_spec(D),
                  pl.BlockSpec((1, 6, D), lambda b, i: (b, 0, 0)),
                  tok_spec(1),
                  pl.BlockSpec((1, D), lambda b, i: (0, 0))] + [const_spec(a) for a in consts],
        out_specs=[s for s, _ in outs],
        out_shape=[t for _, t in outs],
        compiler_params=pltpu.CompilerParams(dimension_semantics=("parallel", "parallel"),
                                             vmem_limit_bytes=VMEM_LIMIT),
        name="inproj",
    )(x, mod, positions.reshape(B, S, 1), g_norm1.reshape(1, D), *consts)


def _flash_chunk_t(qs, ks, vts, bias, ms, ls, s_scr, p_scr, acc_scr, dv):
    nh = len(qs)
    mx = []
    for h in range(nh):
        s = _dot_nt(ks[h], qs[h])
        if bias is not None:
            s = s + bias
        s_scr[h] = s
        mx.append(jnp.max(s, axis=0, keepdims=True))
    new_m, new_l, alphas = [], [], []
    for h in range(nh):
        m_new = jnp.maximum(ms[h], mx[h])
        alpha = jnp.exp2(ms[h] - m_new)
        p = jnp.exp2(s_scr[h] - m_new)
        new_l.append(alpha * ls[h] + jnp.sum(p, axis=0, keepdims=True))
        p_scr[h] = p.astype(BF16)
        new_m.append(m_new)
        alphas.append(alpha)
    for h in range(nh):
        rows = slice(h * dv, (h + 1) * dv)
        acc_scr[rows, :] = alphas[h] * acc_scr[rows, :] + _dot(vts[h], p_scr[h])
    return tuple(new_m), tuple(new_l)


def _pair_masks():
    lane = lax.broadcasted_iota(I32, (1, LANES), 1)
    return lane < 64


def _attn_finish(acc_ref, ls, o_ref, dv):
    for h, l in enumerate(ls):
        acc_ref[h * dv:(h + 1) * dv, :] = acc_ref[h * dv:(h + 1) * dv, :] * (1.0 / l)
    o_ref[0] = acc_ref[...].T.astype(BF16)


def _sortable(x):
    b = pltpu.bitcast(x, I32)
    return b ^ ((b >> 31) & 0x7FFFFFFF)


def _dsa_kernel(qa_ref, qi_ref, wit_ref, ki_ref, ka_ref, vat_ref, o_ref, key_scr, bias_scr, acc_scr,
                s_scr, p_scr, *, tq, n_sel, s_bits):
    i = pl.program_id(1)
    nkv = i + 1
    tk = tq
    qidx = i * tq + lax.broadcasted_iota(I32, (1, tq), 1)
    kidx0 = lax.broadcasted_iota(I32, (tk, 1), 0)
    lo_half = _pair_masks()
    zero = jnp.zeros((), BF16)

    def head_parts(x):
        parts = []
        for h in range(A_HEADS):
            pair = x[:, (h // 2) * LANES:(h // 2 + 1) * LANES]
            parts.append(jnp.where(lo_half, pair, zero) if h % 2 == 0 else jnp.where(lo_half, zero, pair))
        return parts

    qparts = head_parts(qi_ref[0])
    wit = wit_ref[0]
    wrows = [wit[h:h + 1, :] for h in range(IDX_HEADS)]

    def score_body(c, carry):
        ks = pl.multiple_of(c * tk, tk)
        kic = ki_ref[0, pl.ds(ks, tk), :]
        acc = jnp.zeros((tk, tq), F32)
        for h in range(IDX_HEADS):
            acc = acc + jnp.maximum(_dot_nt(kic, qparts[h]), 0.0) * wrows[h]
        acc = jnp.where(acc == 0.0, 0.0, acc)
        key_scr[pl.ds(ks, tk), :] = jnp.where(kidx0 + ks <= qidx, _sortable(acc), INT_MIN)
        return carry

    lax.fori_loop(0, nkv, score_body, 0)

    def count(pred):
        def body(c, cnt):
            ks = pl.multiple_of(c * tk, tk)
            m = pred(key_scr[pl.ds(ks, tk), :], kidx0 + ks).astype(I32)
            return cnt + jnp.sum(m.reshape(tk // 8, 8, tq), axis=0)
        cnt = lax.fori_loop(0, nkv, body, jnp.zeros((8, tq), I32))
        return jnp.sum(cnt, axis=0, keepdims=True)

    def search_cond(st):
        step, _, done, _ = st
        return (step < 32) & (jnp.min(done) == 0)

    def search_body(st):
        step, tu, done, thr_hit = st
        bit = jnp.left_shift(jnp.int32(1), 31 - step)
        cand = (tu | bit) ^ INT_MIN
        cnt = count(lambda k, kidx: k >= cand)
        hit = (cnt == n_sel) & (done == 0)
        thr_hit = jnp.where(hit, cand, thr_hit)
        tu = jnp.where(cnt >= n_sel, tu | bit, tu)
        return step + 1, tu, jnp.where(hit, 1, done), thr_hit

    zrow = jnp.zeros((1, tq), I32)
    _, tu, done, thr_hit = lax.while_loop(search_cond, search_body, (jnp.int32(0), zrow, zrow, zrow))
    thr = jnp.where(done == 1, thr_hit, tu ^ INT_MIN)

    s_all = jnp.int32(1 << s_bits)

    def tie_limits(_):
        cnt_gt = count(lambda k, kidx: k > thr)
        cnt_ge = count(lambda k, kidx: k >= thr)
        need = n_sel - cnt_gt
        excess = (done == 0) & ((cnt_ge - cnt_gt) > need) & (thr != INT_MIN)

        def tie_search(_):
            def index_bit(step, ju):
                cj = ju | jnp.left_shift(jnp.int32(1), s_bits - 1 - step)
                ok = count(lambda k, kidx: (k == thr) & (kidx < cj)) < need
                return jnp.where(ok, cj, ju)
            return lax.fori_loop(0, s_bits, index_bit, zrow)

        ju = lax.cond(jnp.max(excess.astype(I32)) > 0, tie_search, lambda _: zrow, 0)
        return jnp.where(excess, ju, jnp.where(thr == INT_MIN, -1, s_all))

    jcut = lax.cond(jnp.min(done) == 0, tie_limits, lambda _: jnp.full((1, tq), 1 << s_bits, I32), 0)

    def bias_body(c, carry):
        ks = pl.multiple_of(c * tk, tk)
        k = key_scr[pl.ds(ks, tk), :]
        tie = jnp.where(kidx0 + ks <= jcut, 0.0, NEG_BIG)
        bias_scr[pl.ds(ks, tk), :] = jnp.where(k > thr, 0.0, jnp.where(k == thr, tie, NEG_BIG))
        return carry

    lax.fori_loop(0, nkv, bias_body, 0)

    qh = head_parts(qa_ref[0])
    acc_scr[...] = jnp.zeros(acc_scr.shape, F32)

    def att_body(c, carry):
        ms, ls = carry
        ks = pl.multiple_of(c * tk, tk)
        bias = bias_scr[pl.ds(ks, tk), :]
        kcs = [ka_ref[0, pl.ds(ks, tk), (h // 2) * LANES:(h // 2 + 1) * LANES] for h in range(A_HEADS)]
        vts = [vat_ref[0, c, h * A_HEAD_DIM:(h + 1) * A_HEAD_DIM, :] for h in range(A_HEADS)]
        return _flash_chunk_t(qh, kcs, vts, bias, ms, ls, s_scr, p_scr, acc_scr, A_HEAD_DIM)

    m0 = tuple(jnp.full((1, tq), NEG_BIG, F32) for _ in range(A_HEADS))
    l0 = tuple(jnp.zeros((1, tq), F32) for _ in range(A_HEADS))
    _, ls = lax.fori_loop(0, nkv, att_body, (m0, l0))
    _attn_finish(acc_scr, ls, o_ref, A_HEAD_DIM)


def _dsa(qa, qi, wit, ki, ka, vat, tq):
    B, S, _ = qa.shape
    n_sel = min(IDX_TOPK_MAX, S // 4)
    s_bits = int(math.log2(S))
    assert 1 << s_bits == S
    kern = functools.partial(_dsa_kernel, tq=tq, n_sel=n_sel, s_bits=s_bits)
    blk = lambda w: pl.BlockSpec((1, tq, w), lambda b, i: (b, i, 0))
    full = lambda w: pl.BlockSpec((1, S, w), lambda b, i: (b, 0, 0))
    return pl.pallas_call(
        kern,
        grid=(B, S // tq),
        in_specs=[blk(A_W), blk(IDX_HEADS * IDX_DIM),
                  pl.BlockSpec((1, IDX_HEADS, tq), lambda b, i: (b, 0, i)),
                  full(LANES), full(A_W),
                  pl.BlockSpec((1, S // tq, A_W, tq), lambda b, i: (b, 0, 0, 0))],
        out_specs=blk(A_W),
        out_shape=jax.ShapeDtypeStruct((B, S, A_W), BF16),
        scratch_shapes=[pltpu.VMEM((S, tq), I32), pltpu.VMEM((S, tq), F32), pltpu.VMEM((A_W, tq), F32),
                        pltpu.VMEM((A_HEADS, tq, tq), F32), pltpu.VMEM((A_HEADS, tq, tq), BF16)],
        compiler_params=pltpu.CompilerParams(dimension_semantics=("parallel", "arbitrary"),
                                             vmem_limit_bytes=VMEM_LIMIT),
        name="dsa",
    )(qa, qi, wit, ki, ka, vat)


def _mla_kernel(q_ref, k_ref, vt_ref, o_ref, acc_scr, s_scr, p_scr, *, tq):
    i = pl.program_id(1)
    tk = tq
    q = q_ref[0]
    qh = [q[:, h * M_HEAD_PAD:(h + 1) * M_HEAD_PAD] for h in range(M_HEADS)]
    kidx = lax.broadcasted_iota(I32, (tk, tq), 0)
    qidx = lax.broadcasted_iota(I32, (tk, tq), 1)
    diag_bias = jnp.where(kidx <= qidx, 0.0, NEG_BIG)
    acc_scr[...] = jnp.zeros(acc_scr.shape, F32)

    def step(c, carry, bias):
        ms, ls = carry
        ks = pl.multiple_of(c * tk, tk)
        kcs = [k_ref[0, pl.ds(ks, tk), h * M_HEAD_PAD:(h + 1) * M_HEAD_PAD] for h in range(M_HEADS)]
        vts = [vt_ref[0, c, h * M_V:(h + 1) * M_V, :] for h in range(M_HEADS)]
        return _flash_chunk_t(qh, kcs, vts, bias, ms, ls, s_scr, p_scr, acc_scr, M_V)

    m0 = tuple(jnp.full((1, tq), NEG_BIG, F32) for _ in range(M_HEADS))
    l0 = tuple(jnp.zeros((1, tq), F32) for _ in range(M_HEADS))
    carry = lax.fori_loop(0, i, functools.partial(step, bias=None), (m0, l0))
    _, ls = step(i, carry, diag_bias)
    _attn_finish(acc_scr, ls, o_ref, M_V)


def _mla(qm, km, vmt, tq):
    B, S, W = qm.shape
    kern = functools.partial(_mla_kernel, tq=tq)
    return pl.pallas_call(
        kern,
        grid=(B, S // tq),
        in_specs=[pl.BlockSpec((1, tq, W), lambda b, i: (b, i, 0)),
                  pl.BlockSpec((1, S, W), lambda b, i: (b, 0, 0)),
                  pl.BlockSpec((1, S // tq, M_W, tq), lambda b, i: (b, 0, 0, 0))],
        out_specs=pl.BlockSpec((1, tq, M_W), lambda b, i: (b, i, 0)),
        out_shape=jax.ShapeDtypeStruct((B, S, M_W), BF16),
        scratch_shapes=[pltpu.VMEM((M_W, tq), F32), pltpu.VMEM((M_HEADS, tq, tq), F32),
                        pltpu.VMEM((M_HEADS, tq, tq), BF16)],
        compiler_params=pltpu.CompilerParams(dimension_semantics=("parallel", "arbitrary"),
                                             vmem_limit_bytes=VMEM_LIMIT),
        name="mla",
    )(qm, km, vmt)


def _merge_kernel(x_ref, mod_ref, ya_ref, ym_ref, sga_ref, sgm_ref, woa_ref, wom_ref, wout_ref, g2_ref,
                  x1_ref, h2_ref):
    x = x_ref[0]
    mod = mod_ref[0]
    gt1, sh2, sc2 = mod[2:3, :], mod[3:4, :], mod[4:5, :]
    ya = _dot(ya_ref[0], woa_ref[...])
    ym = _dot(ym_ref[0], wom_ref[...])
    mixed = sga_ref[0].astype(F32) * ya + sgm_ref[0].astype(F32) * ym
    x1 = x + gt1 * _dot(mixed.astype(BF16), wout_ref[...])
    x1_ref[0] = x1
    D = x.shape[-1]
    r = lax.rsqrt(jnp.sum(x1 * x1, axis=-1, keepdims=True) * (1.0 / D) + EPS)
    h2_ref[0] = ((x1 * r * g2_ref[...]) * (1.0 + sc2) + sh2).astype(BF16)


def _merge(x, mod, ya, ym, sga, sgm, w_o_a, w_o_m, w_out, g_norm2, tm):
    B, S, D = x.shape
    tok = lambda w: pl.BlockSpec((1, tm, w), lambda b, i: (b, i, 0))
    cst = lambda a: pl.BlockSpec(a.shape, lambda b, i: (0, 0))
    woa, wom, wout = w_o_a.astype(BF16), w_o_m.astype(BF16), w_out.astype(BF16)
    g2 = g_norm2.reshape(1, D)
    return pl.pallas_call(
        _merge_kernel,
        grid=(B, S // tm),
        in_specs=[tok(D), pl.BlockSpec((1, 6, D), lambda b, i: (b, 0, 0)), tok(A_W), tok(M_W), tok(D), tok(D),
                  cst(woa), cst(wom), cst(wout), cst(g2)],
        out_specs=[tok(D), tok(D)],
        out_shape=[jax.ShapeDtypeStruct((B, S, D), F32), jax.ShapeDtypeStruct((B, S, D), BF16)],
        compiler_params=pltpu.CompilerParams(dimension_semantics=("parallel", "parallel"),
                                             vmem_limit_bytes=VMEM_LIMIT),
        name="merge",
    )(x, mod, ya, ym, sga, sgm, woa, wom, wout, g2)


def _top_rows(x, k):
    n = x.shape[0]
    iota = lax.broadcasted_iota(I32, x.shape, 0)
    rank = jnp.full(x.shape, float(k), F32)
    vals = []
    for r in range(k):
        m = jnp.max(x, axis=0, keepdims=True)
        idx = jnp.min(jnp.where(x == m, iota, n), axis=0, keepdims=True)
        hit = iota == idx
        vals.append(m)
        rank = jnp.where(hit, float(r), rank)
        x = jnp.where(hit, -jnp.inf, x)
    return vals, rank


def _dup_bf16(x):
    hi = pltpu.bitcast(x.astype(BF16).astype(F32), I32)
    return hi | lax.shift_right_logical(hi, jnp.int32(16))


def _router_kernel(h2_ref, wq_ref, sk_ref, e0_ref, cnt_ref, e1_ref, rank_ref, cand_scr):
    q = _dot(h2_ref[...], wq_ref[...])
    cand_scr[...] = jnp.full(cand_scr.shape, -jnp.inf, F32)
    cell_id = lax.broadcasted_iota(I32, cand_scr.shape, 0)
    cell_row = jnp.zeros(cand_scr.shape, F32)
    for r, (a, b) in enumerate(_PEER_CELLS):
        if a > 0 and b == 0:
            cell_row = cell_row + jnp.where(cell_id >= r, 1.0, 0.0)
    for h in range(P_HEADS):
        sts, tops, ranks = [], [], []
        for p in range(2):
            j = 2 * h + p
            st = _dot3_nt(sk_ref[j], q[:, j * LANES:(j + 1) * LANES])
            vals, rank = _top_rows(st, P_TOPK)
            sts.append(st)
            tops.append(vals)
            ranks.append(rank)
        for r, (a, b) in enumerate(_PEER_CELLS):
            cand_scr[r:r + 1, :] = tops[0][a] + tops[1][b]
        best, cell_rank = _top_rows(cand_scr[...], P_TOPK)
        z = jnp.zeros_like(best[0])
        for v in best:
            z = z + jnp.exp(v - best[0])
        picked = jnp.where(cell_rank < float(P_TOPK), 1.0, 0.0)
        cnt = jnp.zeros(ranks[0].shape, F32)
        for a in range(P_TOPK):
            cnt_a = jnp.sum(jnp.where(cell_row == float(a), picked, 0.0), axis=0, keepdims=True)
            cnt = jnp.where(ranks[0] == float(a), cnt_a, cnt)
        rows = slice(h * P_NKEYS, (h + 1) * P_NKEYS)
        outs = ((e0_ref, _dup_bf16(jnp.exp(sts[0] - tops[0][0]))), (cnt_ref, _dup_bf16(cnt)),
                (e1_ref, (jnp.exp(sts[1] - tops[1][0]) * (1.0 / z)).astype(BF16)),
                (rank_ref, ranks[1].astype(BF16)))
        for ref, val in outs:
            for lc in range(ref.shape[0]):
                ref[lc, rows, :] = val[:, lc * LANES:(lc + 1) * LANES]


def _router(h2, w_peer_q, peer_subkeys, tm):
    T, D = h2.shape
    wq = w_peer_q.astype(BF16)
    sk = peer_subkeys.reshape(P_HEADS * 2, P_NKEYS, P_DKEY // 2)
    nrow = P_HEADS * P_NKEYS
    ncell_pad = -(-len(_PEER_CELLS) // 8) * 8
    out = lambda dt: (pl.BlockSpec((tm // LANES, nrow, LANES), lambda i: (i, 0, 0)),
                      jax.ShapeDtypeStruct((T // LANES, nrow, LANES), dt))
    outs = [out(I32), out(I32), out(BF16), out(BF16)]
    return pl.pallas_call(
        _router_kernel,
        grid=(T // tm,),
        in_specs=[pl.BlockSpec((tm, D), lambda i: (i, 0)),
                  pl.BlockSpec(wq.shape, lambda i: (0, 0)),
                  pl.BlockSpec(sk.shape, lambda i: (0, 0, 0))],
        out_specs=[s for s, _ in outs],
        out_shape=[t for _, t in outs],
        scratch_shapes=[pltpu.VMEM((ncell_pad, tm), F32)],
        compiler_params=pltpu.CompilerParams(dimension_semantics=("parallel",),
                                             vmem_limit_bytes=VMEM_LIMIT),
        name="router",
    )(h2, wq, sk)


def _gelu_tanh(x):
    return 0.5 * x * (1.0 + jnp.tanh(0.7978845608028654 * (x + 0.044715 * (x * x * x))))


BF16_ROWS = 16


def _experts_kernel(h2_ref, e0_ref, cnt_ref, e1_ref, rank_ref, u_ref, vt_ref, x1_ref, mod_ref, o_ref,
                    acc_scr, a0_scr, a1_scr, c0_scr, c1_scr, *, rows_per_step, n_tiles):
    j = pl.program_id(1)
    tm = h2_ref.shape[0]
    ngrp = P_NKEYS // BF16_ROWS

    @pl.when(j == 0)
    def _():
        acc_scr[...] = jnp.zeros(acc_scr.shape, F32)
        a1_scr[...] = jnp.zeros(a1_scr.shape, F32)
        c0_scr[...] = jnp.zeros(c0_scr.shape, BF16)

    gate_tile = jnp.clip(j - 1, 0, n_tiles - 1)

    def stages(a_new, a_old, c_done, c_new):
        zero = jnp.zeros((), BF16)
        acc_scr[...] += _dot(vt_ref[0], c_done[...])
        a_new[...] = _dot_nt(u_ref[...], h2_ref[...])
        for r in range(rows_per_step):
            rows = slice(r * P_NKEYS, (r + 1) * P_NKEYS)
            i_row = gate_tile * rows_per_step + r
            for lc in range(tm // LANES):
                cols = slice(lc * LANES, (lc + 1) * LANES)
                g = None
                for h in range(P_HEADS):
                    row = pl.ds(h * P_NKEYS + i_row, 1)

                    def bcast(ref):
                        w = jnp.broadcast_to(ref[lc, row, :], (BF16_ROWS // 2, LANES))
                        return jnp.concatenate([pltpu.bitcast(w, BF16)] * ngrp, axis=0)

                    e0 = bcast(e0_ref)
                    cnt = bcast(cnt_ref)
                    rank = rank_ref[lc, h * P_NKEYS:(h + 1) * P_NKEYS, :]
                    e1 = e1_ref[lc, h * P_NKEYS:(h + 1) * P_NKEYS, :]
                    t = e0 * jnp.where(rank < cnt, e1, zero)
                    g = t if g is None else g + t
                act = _gelu_tanh(a_old[rows, cols]).astype(BF16)
                c_new[rows, cols] = g * act

    @pl.when(j % 2 == 0)
    def _():
        stages(a0_scr, a1_scr, c0_scr, c1_scr)

    @pl.when(j % 2 == 1)
    def _():
        stages(a1_scr, a0_scr, c1_scr, c0_scr)

    @pl.when(j == n_tiles + 1)
    def _():
        gt2 = mod_ref[0][5:6, :]
        o_ref[...] = x1_ref[...] + gt2 * acc_scr[...].T


def _experts(h2, e0, cnt, e1, rank, peer_u, peer_v, x1, mod, seq, tm, te):
    T, D = h2.shape
    ub = peer_u.astype(BF16)
    nt = P_NEXP // te
    vt = peer_v.astype(BF16).reshape(nt, te, D).transpose(0, 2, 1)
    nrow = e0.shape[1]
    blocks_per_seq = seq // tm
    kern = functools.partial(_experts_kernel, rows_per_step=te // P_NKEYS, n_tiles=nt)
    fac = pl.BlockSpec((tm // LANES, nrow, LANES), lambda i, j: (i, 0, 0))
    return pl.pallas_call(
        kern,
        grid=(T // tm, nt + 2),
        in_specs=[pl.BlockSpec((tm, D), lambda i, j: (i, 0)),
                  fac, fac, fac, fac,
                  pl.BlockSpec((te, D), lambda i, j: (jnp.minimum(j, nt - 1), 0)),
                  pl.BlockSpec((1, D, te), lambda i, j: (jnp.maximum(j - 2, 0), 0, 0)),
                  pl.BlockSpec((tm, D), lambda i, j: (i, 0)),
                  pl.BlockSpec((1, 6, D), lambda i, j: (i // blocks_per_seq, 0, 0))],
        out_specs=pl.BlockSpec((tm, D), lambda i, j: (i, 0)),
        out_shape=jax.ShapeDtypeStruct((T, D), F32),
        scratch_shapes=[pltpu.VMEM((D, tm), F32), pltpu.VMEM((te, tm), F32), pltpu.VMEM((te, tm), F32),
                        pltpu.VMEM((te, tm), BF16), pltpu.VMEM((te, tm), BF16)],
        compiler_params=pltpu.CompilerParams(dimension_semantics=("parallel", "arbitrary"),
                                             vmem_limit_bytes=VMEM_LIMIT),
        name="experts",
    )(h2, e0, cnt, e1, rank, ub, vt, x1, mod)


def _tile(n, pref):
    t = min(n, pref)
    assert n % t == 0
    return t


def _layer(x, mod, positions, g_norm1, g_norm2, w_in, g_a_q, g_a_k, g_idx_k, g_mq_a, w_mq_up, g_mkv_a,
           w_mkv_up, g_m_q, g_m_k, w_o_a, w_o_m, w_out, w_peer_q, peer_subkeys, peer_u, peer_v):
    B, S, D = x.shape
    T = B * S
    tq = _tile(S, 256)
    qa, ka, vat, qi, ki, wit, qm, km, vmt, sga, sgm = _inproj(
        x, mod, positions, g_norm1, w_in, g_a_q, g_a_k, g_idx_k, g_mq_a, w_mq_up, g_mkv_a, w_mkv_up,
        g_m_q, g_m_k, tq)
    ya = _dsa(qa, qi, wit, ki, ka, vat, tq)
    ym = _mla(qm, km, vmt, tq)
    x1, h2 = _merge(x, mod, ya, ym, sga, sgm, w_o_a, w_o_m, w_out, g_norm2, _tile(S, 512))
    h2f = h2.reshape(T, D)
    e0, cnt, e1, rank = _router(h2f, w_peer_q, peer_subkeys, _tile(T, 256))
    out = _experts(h2f, e0, cnt, e1, rank, peer_u, peer_v, x1.reshape(T, D), mod, S, _tile(S, 512), 512)
    return out.reshape(B, S, D)


def kernel(x, c, positions, g_norm1, g_norm2, w_ada, b_ada, w_in, g_a_q, g_a_k, g_idx_k, g_mq_a, w_mq_up,
           g_mkv_a, w_mkv_up, g_m_q, g_m_k, w_o_a, w_o_m, w_out, w_peer_q, peer_subkeys, peer_u, peer_v):
    depth = w_in.shape[0]
    B, D = c.shape
    for i in range(depth):
        mod = _adaln(c, w_ada[i], b_ada[i]).reshape(B, 6, D)
        x = _layer(x, mod, positions, g_norm1[i], g_norm2[i], w_in[i], g_a_q[i], g_a_k[i], g_idx_k[i],
                   g_mq_a[i], w_mq_up[i], g_mkv_a[i], w_mkv_up[i], g_m_q[i], g_m_k[i], w_o_a[i], w_o_m[i],
                   w_out[i], w_peer_q[i], peer_subkeys[i], peer_u[i], peer_v[i])
    return x
```
